```python
import math
import jax, jax.numpy as jnp
from jax import lax
import numpy as np

D_MODEL = 1024
BATCH = 16
SEQ = 2048
DEPTH = 1

CHUNK = 64
D_MIX = D_MODEL
D_HGRN = D_MIX // 2
HG_HEADS = 4
HG_DK = D_HGRN // HG_HEADS
D_SSM = D_MIX - D_HGRN
SSM_GROUP = 16
SSM_GROUPS = D_SSM // SSM_GROUP
SSM_STATE = 64
IN_COLS = 4 * D_HGRN + D_SSM
PEER_HEADS = 8
PEER_NKEYS = 128
PEER_EXPERTS = PEER_NKEYS * PEER_NKEYS
PEER_DKEY = 256
PEER_HALF = PEER_DKEY // 2
PEER_TOPK = 16
PEER_TOKEN_BLOCK = 128
DEEP_ALPHA = (2.0 * DEPTH) ** 0.25
DEEP_BETA = (8.0 * DEPTH) ** -0.25
LN_EPS = 1e-5
RMS_EPS = 1e-6

kernel_name = 'hymba_hgrn2_s5_peer_deepnorm_adaln'


def layer_norm(x, w, b):
    xf = x.astype(jnp.float32)
    mu = jnp.mean(xf, axis=-1, keepdims=True)
    var = jnp.mean(jnp.square(xf - mu), axis=-1, keepdims=True)
    y = (xf - mu) * lax.rsqrt(var + LN_EPS) * w.astype(jnp.float32) + b.astype(jnp.float32)
    return y.astype(x.dtype)


def rms_norm(x, w):
    xf = x.astype(jnp.float32)
    return xf * lax.rsqrt(jnp.mean(jnp.square(xf), axis=-1, keepdims=True) + RMS_EPS) * w.astype(jnp.float32)


def hgrn2_group(q, f_logit, i, g, lb, norm_w):
    B, S, _ = q.shape
    nc = S // CHUNK
    f32 = jnp.float32
    q = jax.nn.silu(q.astype(f32))
    f = lb + (1.0 - lb) * jax.nn.sigmoid(f_logit.astype(f32))
    log_f = jnp.log(f)
    k = 1.0 - f
    v = i.astype(f32)

    def to_chunks(t):
        return t.reshape(B, nc, CHUNK, HG_HEADS, HG_DK).transpose(1, 0, 3, 2, 4)

    qc, kc, vc, lc = to_chunks(q), to_chunks(k), to_chunks(v), to_chunks(log_f)
    causal = jnp.tril(jnp.ones((CHUNK, CHUNK), dtype=bool))[:, :, None]

    def step(state, inp):
        q_c, k_c, v_c, l_c = inp
        b = jnp.cumsum(l_c, axis=2)
        diff = b[:, :, :, None, :] - b[:, :, None, :, :]
        decay = jnp.exp(jnp.where(causal, diff, -jnp.inf))
        scores = jnp.einsum('bhtk,bhtsk,bhsk->bhts', q_c, decay, k_c)
        o = (jnp.einsum('bhts,bhsv->bhtv', scores, v_c)
             + jnp.einsum('bhtk,bhkv->bhtv', q_c * jnp.exp(b), state))
        b_last = b[:, :, -1:, :]
        state = (jnp.exp(b_last[:, :, 0, :])[..., None] * state
                 + jnp.einsum('bhsk,bhsv->bhkv', k_c * jnp.exp(b_last - b), v_c))
        return state, o

    s0 = jnp.zeros((B, HG_HEADS, HG_DK, HG_DK), f32)
    _, o = lax.scan(step, s0, (qc, kc, vc, lc))
    o = o.transpose(1, 0, 3, 2, 4).reshape(B, S, HG_HEADS, HG_DK)
    o = rms_norm(o, norm_w.reshape(HG_HEADS, HG_DK))
    o = o * jax.nn.silu(g.astype(f32).reshape(B, S, HG_HEADS, HG_DK))
    return o.reshape(B, S, D_HGRN)


def s5_group(u, a_re, a_im, log_dt, b_re, b_im, c_re, c_im, d_skip, glu_w, glu_b, norm_w):
    B, S, _ = u.shape
    f32 = jnp.float32
    uf = u.astype(f32).reshape(B, S, SSM_GROUPS, SSM_GROUP)
    ar, ai = a_re.astype(f32), a_im.astype(f32)
    dt = jnp.exp(log_dt.astype(f32))[:, None]
    mag = jnp.exp(ar * dt)
    lam_re = mag * jnp.cos(ai * dt)
    lam_im = mag * jnp.sin(ai * dt)
    den = ar * ar + ai * ai
    nr, ni = lam_re - 1.0, lam_im
    z_re = (nr * ar + ni * ai) / den
    z_im = (ni * ar - nr * ai) / den
    br, bi = b_re.astype(f32), b_im.astype(f32)
    bb_re = z_re[..., None] * br - z_im[..., None] * bi
    bb_im = z_re[..., None] * bi + z_im[..., None] * br
    bu_re = jnp.einsum('bsgi,gpi->bsgp', uf, bb_re)
    bu_im = jnp.einsum('bsgi,gpi->bsgp', uf, bb_im)
    lam_re_s = jnp.broadcast_to(lam_re, (S, SSM_GROUPS, SSM_STATE))
    lam_im_s = jnp.broadcast_to(lam_im, (S, SSM_GROUPS, SSM_STATE))

    def combine(e_i, e_j):
        ar_i, ai_i, br_i, bi_i = e_i
        ar_j, ai_j, br_j, bi_j = e_j
        return (ar_j * ar_i - ai_j * ai_i,
                ar_j * ai_i + ai_j * ar_i,
                ar_j * br_i - ai_j * bi_i + br_j,
                ar_j * bi_i + ai_j * br_i + bi_j)

    def scan_one(bre, bim):
        _, _, xr, xi = lax.associative_scan(combine, (lam_re_s, lam_im_s, bre, bim), axis=0)
        return xr, xi

    x_re, x_im = jax.vmap(scan_one)(bu_re, bu_im)
    y = (jnp.einsum('bsgp,gip->bsgi', x_re, c_re.astype(f32))
         - jnp.einsum('bsgp,gip->bsgi', x_im, c_im.astype(f32))
         + d_skip.astype(f32) * uf)
    y = jax.nn.gelu(y.reshape(B, S, D_SSM), approximate=False)
    y = y * jax.nn.sigmoid(y @ glu_w.astype(f32) + glu_b.astype(f32))
    return rms_norm(y, norm_w)


def peer_ffn(h, w_q, sub_keys, u_tab, v_tab):
    B, S, D = h.shape
    T = B * S
    f32 = jnp.float32
    ht = h.reshape(T, D)
    q = (ht @ w_q).astype(f32).reshape(T, PEER_HEADS, 2, PEER_HALF)
    scores = jnp.einsum('thpc,hpnc->thpn', q, sub_keys.astype(f32))
    top_s, top_i = lax.top_k(scores, PEER_TOPK)
    cand = (top_s[:, :, 0, :, None] + top_s[:, :, 1, None, :]).reshape(T, PEER_HEADS, PEER_TOPK * PEER_TOPK)
    best_s, best_c = lax.top_k(cand, PEER_TOPK)
    i1 = jnp.take_along_axis(top_i[:, :, 0], best_c // PEER_TOPK, axis=-1)
    i2 = jnp.take_along_axis(top_i[:, :, 1], best_c % PEER_TOPK, axis=-1)
    hk = PEER_HEADS * PEER_TOPK
    expert = (i1 * PEER_NKEYS + i2).reshape(T, hk)
    gate = jax.nn.softmax(best_s, axis=-1).reshape(T, hk)
    nb = T // PEER_TOKEN_BLOCK

    def block(args):
        hb, eb, gb = args
        u_sel = jnp.take(u_tab, eb, axis=0)
        z = jnp.einsum('ted,td->te', u_sel, hb)
        a = jax.nn.gelu(z.astype(f32), approximate=False) * gb
        v_sel = jnp.take(v_tab, eb, axis=0)
        return jnp.einsum('te,ted->td', a.astype(v_sel.dtype), v_sel)

    out = lax.map(block, (ht.reshape(nb, PEER_TOKEN_BLOCK, D),
                          expert.reshape(nb, PEER_TOKEN_BLOCK, hk),
                          gate.reshape(nb, PEER_TOKEN_BLOCK, hk)))
    return out.reshape(B, S, D).astype(h.dtype)


def setup_inputs(seed: int = 0) -> dict:
    key = jax.random.key(seed)
    ks = jax.random.split(key, 32)
    f32 = jnp.float32
    L, D = DEPTH, D_MODEL

    def nrm(k, shape, s):
        return jax.random.normal(k, shape, f32) * s

    n_idx = jnp.arange(SSM_STATE, dtype=f32)
    return {
        'x': nrm(ks[0], (BATCH, SEQ, D), 1.0),
        'c': nrm(ks[1], (BATCH, D), 1.0),
        'ada_w': nrm(ks[2], (L, D, 6 * D), 0.5 * D ** -0.5),
        'ada_b': nrm(ks[3], (L, 6 * D), 0.02),
        'w_in': nrm(ks[4], (L, D, IN_COLS), D ** -0.5),
        'hg_lower_bound': nrm(ks[5], (DEPTH + 1, D_HGRN), 1.0),
        'hg_norm_w': 1.0 + nrm(ks[6], (L, D_HGRN), 0.02),
        'ssm_a_re': -0.5 + nrm(ks[7], (L, SSM_GROUPS, SSM_STATE), 0.01),
        'ssm_a_im': jnp.pi * n_idx + nrm(ks[8], (L, SSM_GROUPS, SSM_STATE), 0.01),
        'ssm_log_dt': jax.random.uniform(ks[9], (L, SSM_GROUPS), f32, math.log(1e-3), math.log(1e-1)),
        'ssm_b_re': nrm(ks[10], (L, SSM_GROUPS, SSM_STATE, SSM_GROUP), (2.0 * SSM_GROUP) ** -0.5),
        'ssm_b_im': nrm(ks[11], (L, SSM_GROUPS, SSM_STATE, SSM_GROUP), (2.0 * SSM_GROUP) ** -0.5),
        'ssm_c_re': nrm(ks[12], (L, SSM_GROUPS, SSM_GROUP, SSM_STATE), SSM_STATE ** -0.5),
        'ssm_c_im': nrm(ks[13], (L, SSM_GROUPS, SSM_GROUP, SSM_STATE), SSM_STATE ** -0.5),
        'ssm_d': nrm(ks[14], (L, SSM_GROUPS, SSM_GROUP), 1.0),
        'ssm_glu_w': nrm(ks[15], (L, D_SSM, D_SSM), D_SSM ** -0.5),
        'ssm_glu_b': nrm(ks[16], (L, D_SSM), 0.02),
        'ssm_norm_w': 1.0 + nrm(ks[17], (L, D_SSM), 0.02),
        'w_out': nrm(ks[18], (L, D_MIX, D), DEEP_BETA * D_MIX ** -0.5),
        'ln1_w': 1.0 + nrm(ks[19], (L, D), 0.02),
        'ln1_b': nrm(ks[20], (L, D), 0.02),
        'peer_w_q': nrm(ks[21], (L, D, PEER_HEADS * PEER_DKEY), D ** -0.5),
        'peer_sub_keys': nrm(ks[22], (L, PEER_HEADS, 2, PEER_NKEYS, PEER_HALF), PEER_HALF ** -0.5),
        'peer_u': nrm(ks[23], (L, PEER_EXPERTS, D), D ** -0.5),
        'peer_v': nrm(ks[24], (L, PEER_EXPERTS, D), DEEP_BETA * PEER_HEADS ** -0.5),
        'ln2_w': 1.0 + nrm(ks[25], (L, D), 0.02),
        'ln2_b': nrm(ks[26], (L, D), 0.02),
    }


def reference(x, c, ada_w, ada_b, w_in, hg_lower_bound, hg_norm_w, ssm_a_re, ssm_a_im, ssm_log_dt,
              ssm_b_re, ssm_b_im, ssm_c_re, ssm_c_im, ssm_d, ssm_glu_w, ssm_glu_b, ssm_norm_w,
              w_out, ln1_w, ln1_b, peer_w_q, peer_sub_keys, peer_u, peer_v, ln2_w, ln2_b):
    lb_all = jnp.cumsum(jax.nn.softmax(hg_lower_bound.astype(jnp.float32), axis=0), axis=0)
    cond = jax.nn.silu(c)
    for l in range(DEPTH):
        mod = cond @ ada_w[l] + ada_b[l]
        sh1, sc1, g1, sh2, sc2, g2 = jnp.split(mod[:, None, :], 6, axis=-1)

        h = x * (1.0 + sc1) + sh1
        proj = h @ w_in[l]
        q, f, i, g, u = jnp.split(proj, [D_HGRN, 2 * D_HGRN, 3 * D_HGRN, 4 * D_HGRN], axis=-1)
        o_hg = hgrn2_group(q, f, i, g, lb_all[l], hg_norm_w[l])
        o_ssm = s5_group(u, ssm_a_re[l], ssm_a_im[l], ssm_log_dt[l], ssm_b_re[l], ssm_b_im[l],
                         ssm_c_re[l], ssm_c_im[l], ssm_d[l], ssm_glu_w[l], ssm_glu_b[l], ssm_norm_w[l])
        mixed = jnp.concatenate([o_hg, o_ssm], axis=-1).astype(x.dtype) @ w_out[l]
        x = layer_norm(DEEP_ALPHA * x + (1.0 + g1) * mixed, ln1_w[l], ln1_b[l])

        h2 = x * (1.0 + sc2) + sh2
        ffn = peer_ffn(h2, peer_w_q[l], peer_sub_keys[l], peer_u[l], peer_v[l])
        x = layer_norm(DEEP_ALPHA * x + (1.0 + g2) * ffn, ln2_w[l], ln2_b[l])
    return x
```

```python
import functools
import math

import jax
import jax.numpy as jnp
from jax import lax
from jax.experimental import pallas as pl
from jax.experimental.pallas import tpu as pltpu

F32 = jnp.float32
BF16 = jnp.bfloat16
I32 = jnp.int32

D_MODEL = 1024
DEPTH = 1
D_HGRN = 512
HG_HEADS = 4
HG_DK = 128
D_SSM = 512
SSM_GROUP = 16
SSM_GROUPS = 32
SSM_STATE = 64
IN_COLS = 4 * D_HGRN + D_SSM
PEER_HEADS = 8
PEER_NKEYS = 128
PEER_EXPERTS = PEER_NKEYS * PEER_NKEYS
PEER_DKEY = 256
PEER_HALF = 128
PEER_TOPK = 16
PEER_HK = PEER_HEADS * PEER_TOPK
DEEP_ALPHA = (2.0 * DEPTH) ** 0.25
LN_EPS = 1e-5
RMS_EPS = 1e-6

LANES = 128
SUBLANES = 8
HG_CHUNK = 16
NEG_BIG = -1e30
VMEM_LIMIT = 48 * 1024 * 1024


def _sigmoid(x):
    return 1.0 / (1.0 + jnp.exp(-x))


def _silu(x):
    return x * _sigmoid(x)


def _gelu_exact(x):
    return 0.5 * x * (1.0 + lax.erf(x * (1.0 / math.sqrt(2.0))))


def _params(sem):
    return pltpu.CompilerParams(dimension_semantics=sem, vmem_limit_bytes=VMEM_LIMIT)


def _mod_kernel(c_ref, w_ref, b_ref, o_ref):
    cond = _silu(c_ref[...])
    o_ref[...] = jnp.dot(cond, w_ref[...], preferred_element_type=F32,
                         precision=lax.Precision.HIGHEST) + b_ref[...]


def _mod_call(c, ada_w, ada_b):
    B, D = c.shape
    n = ada_w.shape[1] // D
    return pl.pallas_call(
        _mod_kernel,
        grid=(n,),
        in_specs=[pl.BlockSpec((B, D), lambda j: (0, 0)),
                  pl.BlockSpec((D, D), lambda j: (0, j)),
                  pl.BlockSpec((1, D), lambda j: (0, j))],
        out_specs=pl.BlockSpec((B, D), lambda j: (0, j)),
        out_shape=jax.ShapeDtypeStruct((B, n * D), F32),
        compiler_params=_params(("arbitrary",)),
        name="mod",
    )(c, ada_w, ada_b.reshape(1, -1))


def _inproj_kernel(x_ref, sc_ref, sh_ref, w_ref, o_ref):
    h = x_ref[0] * (1.0 + sc_ref[0]) + sh_ref[0]
    o_ref[0] = jnp.dot(h.astype(BF16), w_ref[...], preferred_element_type=F32)


def _inproj_call(x, sc1, sh1, w_in_bf16, tm):
    B, S, D = x.shape
    N = w_in_bf16.shape[1]
    return pl.pallas_call(
        _inproj_kernel,
        grid=(B, S // tm),
        in_specs=[pl.BlockSpec((1, tm, D), lambda b, i: (b, i, 0)),
                  pl.BlockSpec((1, 1, D), lambda b, i: (b, 0, 0)),
                  pl.BlockSpec((1, 1, D), lambda b, i: (b, 0, 0)),
                  pl.BlockSpec((D, N), lambda b, i: (0, 0))],
        out_specs=pl.BlockSpec((1, tm, N), lambda b, i: (b, i, 0)),
        out_shape=jax.ShapeDtypeStruct((B, S, N), F32),
        compiler_params=_params(("arbitrary", "arbitrary")),
        name="inproj",
    )(x, sc1, sh1, w_in_bf16)


def _hgrn_kernel(q_ref, f_ref, i_ref, g_ref, lb_ref, nw_ref, o_ref, st_ref, b_ref, *, ts):
    @pl.when(pl.program_id(2) == 0)
    def _():
        st_ref[...] = jnp.zeros_like(st_ref)

    lb = lb_ref[...]
    nw = nw_ref[...]
    f_all = lb + (1.0 - lb) * _sigmoid(f_ref[0])
    lf = jnp.log(f_all)
    r = lax.broadcasted_iota(I32, (ts, ts), 0)
    c = lax.broadcasted_iota(I32, (ts, ts), 1)
    tri = jnp.where((r // HG_CHUNK == c // HG_CHUNK) & (c <= r), 1.0, 0.0).astype(BF16)
    hi = lf.astype(BF16)
    r1 = lf - hi.astype(F32)
    mid = r1.astype(BF16)
    lo = (r1 - mid.astype(F32)).astype(BF16)
    b_ref[...] = (jnp.dot(tri, hi, preferred_element_type=F32)
                  + jnp.dot(tri, mid, preferred_element_type=F32)
                  + jnp.dot(tri, lo, preferred_element_type=F32))

    rows = lax.broadcasted_iota(I32, (HG_CHUNK, 1), 0)

    def chunk(ci, carry):
        r0 = pl.multiple_of(ci * HG_CHUNK, HG_CHUNK)
        sl = pl.ds(r0, HG_CHUNK)
        q = q_ref[0, sl, :]
        v = i_ref[0, sl, :]
        g = g_ref[0, sl, :]
        qs = _silu(q)
        k = 1.0 - (lb + (1.0 - lb) * _sigmoid(f_ref[0, sl, :]))
        b = b_ref[sl, :]
        bl = b[HG_CHUNK - 1:HG_CHUNK, :]
        qh = qs * jnp.exp(b)
        kh = k * jnp.exp(bl - b)
        st = st_ref[...]
        o = lax.dot_general(qh.astype(BF16), st.astype(BF16), (((1,), (1,)), ((), ())),
                            preferred_element_type=F32)
        for s in range(HG_CHUNK):
            e = jnp.exp(jnp.where(rows >= s, b - b[s:s + 1, :], NEG_BIG))
            col = jnp.sum(e * qs * k[s:s + 1, :], axis=1, keepdims=True)
            o = o + col * v[s:s + 1, :]
        ms = jnp.mean(o * o, axis=1, keepdims=True)
        o_ref[0, sl, :] = o * lax.rsqrt(ms + RMS_EPS) * nw * _silu(g)
        upd = lax.dot_general(v.astype(BF16), kh.astype(BF16), (((0,), (0,)), ((), ())),
                              preferred_element_type=F32)
        st_ref[...] = st * jnp.exp(bl) + upd
        return carry

    lax.fori_loop(0, ts // HG_CHUNK, chunk, 0)


def _hgrn_call(proj, lb, norm_w, ts):
    B, S, _ = proj.shape
    H = HG_HEADS

    def col(off):
        return pl.BlockSpec((1, ts, HG_DK), lambda b, h, i, off=off: (b, i, off + h))

    vec = pl.BlockSpec((1, HG_DK), lambda b, h, i: (0, h))
    return pl.pallas_call(
        functools.partial(_hgrn_kernel, ts=ts),
        grid=(B, H, S // ts),
        in_specs=[col(0), col(H), col(2 * H), col(3 * H), vec, vec],
        out_specs=pl.BlockSpec((1, ts, HG_DK), lambda b, h, i: (b, i, h)),
        out_shape=jax.ShapeDtypeStruct((B, S, D_HGRN), F32),
        scratch_shapes=[pltpu.VMEM((HG_DK, HG_DK), F32), pltpu.VMEM((ts, HG_DK), F32)],
        compiler_params=_params(("arbitrary", "arbitrary", "arbitrary")),
        name="hgrn",
    )(proj, proj, proj, proj, lb.reshape(1, -1), norm_w.reshape(1, -1))


S5_CB = D_SSM // LANES
S5_NS = SSM_GROUPS * SSM_STATE
S5_SB = S5_NS // S5_CB


def _s5_kernel(u_ref, lre_ref, lim_ref, bm_ref, cm_ref, d_ref, gw_ref, gb_ref, nw_ref, o_ref,
               ut_ref, bu_ref, xs_ref, y_ref, *, nb, tt):
    @pl.when(pl.program_id(0) == 0)
    def _():
        xs_ref[...] = jnp.zeros_like(xs_ref)

    for b in range(nb):
        for cb in range(S5_CB):
            ut_ref[cb, pl.ds(b, tt, stride=nb), :] = u_ref[b, :, cb * LANES:(cb + 1) * LANES]
    for cb in range(S5_CB):
        res = jnp.dot(ut_ref[cb].astype(BF16), bm_ref[cb], preferred_element_type=F32)
        bu_ref[:, cb * S5_SB:(cb + 1) * S5_SB] = res[:, :S5_SB]
        bu_ref[:, S5_NS + cb * S5_SB:S5_NS + (cb + 1) * S5_SB] = res[:, S5_SB:]

    for sb in range(S5_CB):
        re = slice(sb * S5_SB, (sb + 1) * S5_SB)
        im = slice(S5_NS + sb * S5_SB, S5_NS + (sb + 1) * S5_SB)
        lr = lre_ref[:, re]
        li = lim_ref[:, re]

        def step(t, carry, re=re, im=im, lr=lr, li=li):
            xr, xi = carry
            rows = pl.ds(pl.multiple_of(t * nb, nb), nb)
            nxr = lr * xr - li * xi + bu_ref[rows, re]
            nxi = lr * xi + li * xr + bu_ref[rows, im]
            bu_ref[rows, re] = nxr
            bu_ref[rows, im] = nxi
            return nxr, nxi

        xr, xi = lax.fori_loop(0, tt, step, (xs_ref[:, re], xs_ref[:, im]), unroll=4)
        xs_ref[:, re] = xr
        xs_ref[:, im] = xi

    ys = []
    for cb in range(S5_CB):
        xre = bu_ref[:, cb * S5_SB:(cb + 1) * S5_SB].astype(BF16)
        xim = bu_ref[:, S5_NS + cb * S5_SB:S5_NS + (cb + 1) * S5_SB].astype(BF16)
        ys.append(jnp.dot(xre, cm_ref[cb, :S5_SB, :], preferred_element_type=F32)
                  + jnp.dot(xim, cm_ref[cb, S5_SB:, :], preferred_element_type=F32)
                  + d_ref[:, cb * LANES:(cb + 1) * LANES] * ut_ref[cb])
    y = _gelu_exact(jnp.concatenate(ys, axis=1))
    z = jnp.dot(y.astype(BF16), gw_ref[...], preferred_element_type=F32) + gb_ref[...]
    y = y * _sigmoid(z)
    y = y * lax.rsqrt(jnp.mean(y * y, axis=1, keepdims=True) + RMS_EPS) * nw_ref[...]
    for cb in range(S5_CB):
        y_ref[cb] = y[:, cb * LANES:(cb + 1) * LANES]
    for b in range(nb):
        for cb in range(S5_CB):
            o_ref[b, :, cb * LANES:(cb + 1) * LANES] = y_ref[cb, pl.ds(b, tt, stride=nb), :]


def _s5_prepare(a_re, a_im, log_dt, b_re, b_im, c_re, c_im):
    ar, ai = a_re.astype(F32), a_im.astype(F32)
    dt = jnp.exp(log_dt.astype(F32))[:, None]
    mag = jnp.exp(ar * dt)
    lam_re = mag * jnp.cos(ai * dt)
    lam_im = mag * jnp.sin(ai * dt)
    den = ar * ar + ai * ai
    nr, ni = lam_re - 1.0, lam_im
    z_re = (nr * ar + ni * ai) / den
    z_im = (ni * ar - nr * ai) / den
    br, bi = b_re.astype(F32), b_im.astype(F32)
    bb_re = z_re[..., None] * br - z_im[..., None] * bi
    bb_im = z_re[..., None] * bi + z_im[..., None] * br
    gl = SSM_GROUPS // S5_CB
    eye = jnp.eye(gl, dtype=F32)

    def in_block(m):
        m = m.reshape(S5_CB, gl, SSM_STATE, SSM_GROUP)
        return jnp.einsum('cgpi,gh->cgihp', m, eye).reshape(S5_CB, gl * SSM_GROUP, gl * SSM_STATE)

    def out_block(m):
        m = m.reshape(S5_CB, gl, SSM_GROUP, SSM_STATE)
        return jnp.einsum('cgip,gh->cgphi', m, eye).reshape(S5_CB, gl * SSM_STATE, gl * SSM_GROUP)

    bm = jnp.concatenate([in_block(bb_re), in_block(bb_im)], axis=2).astype(BF16)
    cm = jnp.concatenate([out_block(c_re.astype(F32)), -out_block(c_im.astype(F32))], axis=1).astype(BF16)
    return lam_re.reshape(1, -1), lam_im.reshape(1, -1), bm, cm


def _s5_call(proj, lam_re, lam_im, bm, cm, d_skip, glu_w_bf16, glu_b, norm_w, tt):
    B, S, _ = proj.shape
    R = tt * B
    ucol = (4 * D_HGRN) // D_SSM
    full = lambda shape: pl.BlockSpec(shape, lambda i: (0,) * len(shape))
    return pl.pallas_call(
        functools.partial(_s5_kernel, nb=B, tt=tt),
        grid=(S // tt,),
        in_specs=[pl.BlockSpec((B, tt, D_SSM), lambda i: (0, i, ucol)),
                  full((1, S5_NS)), full((1, S5_NS)),
                  full((S5_CB, LANES, 2 * S5_SB)), full((S5_CB, 2 * S5_SB, LANES)),
                  full((1, D_SSM)), full((D_SSM, D_SSM)), full((1, D_SSM)), full((1, D_SSM))],
        out_specs=pl.BlockSpec((B, tt, D_SSM), lambda i: (0, i, 0)),
        out_shape=jax.ShapeDtypeStruct((B, S, D_SSM), F32),
        scratch_shapes=[pltpu.VMEM((S5_CB, R, LANES), F32), pltpu.VMEM((R, 2 * S5_NS), F32),
                        pltpu.VMEM((B, 2 * S5_NS), F32), pltpu.VMEM((S5_CB, R, LANES), F32)],
        compiler_params=_params(("arbitrary",)),
        name="s5",
    )(proj, lam_re, lam_im, bm, cm, d_skip.reshape(1, -1), glu_w_bf16, glu_b.reshape(1, -1),
      norm_w.reshape(1, -1))


def _layer_norm(y, w, b):
    mu = jnp.mean(y, axis=-1, keepdims=True)
    yc = y - mu
    var = jnp.mean(yc * yc, axis=-1, keepdims=True)
    return yc * lax.rsqrt(var + LN_EPS) * w + b


def _out_kernel(x_ref, hg_ref, ss_ref, g1_ref, sc2_ref, sh2_ref, wo_ref, lw_ref, lb_ref, wq_ref,
                x1_ref, h2_ref, q2_ref):
    mixed = (jnp.dot(hg_ref[0].astype(BF16), wo_ref[:D_HGRN, :], preferred_element_type=F32)
             + jnp.dot(ss_ref[0].astype(BF16), wo_ref[D_HGRN:, :], preferred_element_type=F32))
    x1 = _layer_norm(DEEP_ALPHA * x_ref[0] + (1.0 + g1_ref[0]) * mixed, lw_ref[...], lb_ref[...])
    x1_ref[0] = x1
    h2 = x1 * (1.0 + sc2_ref[0]) + sh2_ref[0]
    h2_ref[0] = h2
    q2_ref[0] = jnp.dot(h2.astype(BF16), wq_ref[...], preferred_element_type=F32)


def _out_call(x, o_hg, o_ssm, g1, sc2, sh2, w_out_bf16, ln_w, ln_b, w_q_bf16, tm):
    B, S, D = x.shape
    NQ = w_q_bf16.shape[1]
    row = lambda n: pl.BlockSpec((1, tm, n), lambda b, i: (b, i, 0))
    mod = pl.BlockSpec((1, 1, D), lambda b, i: (b, 0, 0))
    full = lambda shape: pl.BlockSpec(shape, lambda b, i: (0,) * len(shape))
    return pl.pallas_call(
        _out_kernel,
        grid=(B, S // tm),
        in_specs=[row(D), row(D_HGRN), row(D_SSM), mod, mod, mod,
                  full((D, D)), full((1, D)), full((1, D)), full((D, NQ))],
        out_specs=[row(D), row(D), row(NQ)],
        out_shape=[jax.ShapeDtypeStruct((B, S, D), F32), jax.ShapeDtypeStruct((B, S, D), F32),
                   jax.ShapeDtypeStruct((B, S, NQ), F32)],
        compiler_params=_params(("arbitrary", "arbitrary")),
        name="outproj",
    )(x, o_hg, o_ssm, g1, sc2, sh2, w_out_bf16, ln_w.reshape(1, -1), ln_b.reshape(1, -1), w_q_bf16)


_CAND_PIECES = ((0, 1, 0, 16, 0), (1, 1, 0, 8, 0), (2, 1, 0, 8, 0), (3, 1, 0, 8, 0),
                (8, 8, 0, 1, 0), (0, 8, 0, 1, 4), (0, 8, 1, 1, 4), (0, 8, 2, 1, 4))


def _topk_kernel(q_ref, k_ref, j_ref, p_ref, g_ref, *, tb):
    nk = PEER_NKEYS
    kiota = lax.broadcasted_iota(I32, (nk, tb), 0)
    ninf = jnp.float32(-jnp.inf)

    def top16(sc):
        vals, idxs = [], []
        for _ in range(PEER_TOPK):
            m = jnp.max(sc, axis=0, keepdims=True)
            idx = jnp.min(jnp.where(sc == m, kiota, nk), axis=0, keepdims=True)
            sc = jnp.where(kiota == idx, ninf, sc)
            vals.append(m)
            idxs.append(idx)
        return jnp.concatenate(vals, axis=0), jnp.concatenate(idxs, axis=0)

    flat_rows = []
    for a0, na, b0, nb, _ in _CAND_PIECES:
        for a in range(a0, a0 + na):
            for b in range(b0, b0 + nb):
                flat_rows.append(a * PEER_TOPK + b)
    ncand = len(flat_rows)
    riota = lax.broadcasted_iota(I32, (ncand, 1), 0)
    cflat = jnp.zeros((ncand, 1), I32)
    for rix, fv in enumerate(flat_rows):
        cflat = jnp.where(riota == rix, fv, cflat)

    e_rows, g_rows = [], []
    for h in range(PEER_HEADS):
        tops, topi = [], []
        for p in range(2):
            hp = 2 * h + p
            qhp = q_ref[:, hp * PEER_HALF:(hp + 1) * PEER_HALF].astype(BF16)
            sc = lax.dot_general(k_ref[hp], qhp, (((1,), (1,)), ((), ())),
                                 preferred_element_type=F32)
            v_, i_ = top16(sc)
            tops.append(v_)
            topi.append(i_)
        s1, s2 = tops
        i1, i2 = topi
        cv, ce = [], []
        for a0, na, b0, nb, mask_below in _CAND_PIECES:
            val = s1[a0:a0 + na, :] + s2[b0:b0 + nb, :]
            eid = i1[a0:a0 + na, :] * PEER_NKEYS + i2[b0:b0 + nb, :]
            if mask_below:
                prow = lax.broadcasted_iota(I32, (max(na, nb), 1), 0)
                val = jnp.where(prow < mask_below, ninf, val)
            cv.append(val)
            ce.append(eid)
        cand = jnp.concatenate(cv, axis=0)
        cexp = jnp.concatenate(ce, axis=0)
        bv, be = [], []
        for _ in range(PEER_TOPK):
            m = jnp.max(cand, axis=0, keepdims=True)
            cm = jnp.min(jnp.where(cand == m, cflat, PEER_TOPK * PEER_TOPK), axis=0, keepdims=True)
            sel = cflat == cm
            be.append(jnp.max(jnp.where(sel, cexp, -1), axis=0, keepdims=True))
            cand = jnp.where(sel, ninf, cand)
            bv.append(m)
        best = jnp.concatenate(bv, axis=0)
        ex = jnp.exp(best - best[0:1, :])
        g_rows.append(ex / jnp.sum(ex, axis=0, keepdims=True))
        e_rows.append(jnp.concatenate(be, axis=0))
    expert = jnp.concatenate(e_rows, axis=0)
    gate = jnp.concatenate(g_rows, axis=0)
    j_ref[...] = jnp.transpose((expert >> 1).astype(F32)).astype(I32)
    p_ref[...] = jnp.transpose((expert & 1).astype(F32))
    g_ref[...] = jnp.transpose(gate)


def _topk_call(q2, keys_bf16, tb):
    T = q2.shape[0]
    out = lambda dt: jax.ShapeDtypeStruct((T, PEER_HK), dt)
    ospec = pl.BlockSpec((tb, PEER_HK), lambda i: (i, 0))
    return pl.pallas_call(
        functools.partial(_topk_kernel, tb=tb),
        grid=(T // tb,),
        in_specs=[pl.BlockSpec((tb, PEER_HEADS * PEER_DKEY), lambda i: (i, 0)),
                  pl.BlockSpec((2 * PEER_HEADS, PEER_NKEYS, PEER_HALF), lambda i: (0, 0, 0))],
        out_specs=[ospec, ospec, ospec],
        out_shape=[out(I32), out(F32), out(F32)],
        compiler_params=_params(("arbitrary",)),
        name="topk",
    )(q2, keys_bf16)


PEER_ROWS = D_MODEL // LANES
SLAB_ROWS = 2 * PEER_ROWS
SLABS_PER_TILE = 16
N_TILES = PEER_HK // SLABS_PER_TILE
EXP_COLS = PEER_HK * SLAB_ROWS


def _expand_matrix():
    r = lax.broadcasted_iota(I32, (PEER_HK, EXP_COLS), 0)
    c = lax.broadcasted_iota(I32, (PEER_HK, EXP_COLS), 1)
    return jnp.where(c // SLAB_ROWS == r, 1.0, 0.0).astype(BF16)


def _diag_target(par_rep):
    lane = lax.broadcasted_iota(I32, par_rep.shape, 1)
    return (lane % SLAB_ROWS).astype(F32) - PEER_ROWS * par_rep


def _peer_u_kernel(j_sm, h_ref, p_ref, g_ref, u_ref, a_ref, tgt_ref, zs_ref, *, tb):
    expand = _expand_matrix()
    par_rep = jnp.dot(p_ref[...].astype(BF16), expand, preferred_element_type=F32)
    tgt_ref[...] = _diag_target(par_rep)
    sub = lax.broadcasted_iota(I32, (PEER_ROWS, EXP_COLS), 0).astype(F32)

    def token(t, carry):
        hb = h_ref[t].astype(BF16)
        gs = []
        for c in range(N_TILES):
            w = jnp.concatenate([u_ref[j_sm[t, c * SLABS_PER_TILE + m]]
                                 for m in range(SLABS_PER_TILE)], axis=0)
            gs.append(lax.dot_general(hb, w, (((1,), (1,)), ((), ())),
                                      preferred_element_type=F32))
        g = jnp.concatenate(gs, axis=1)
        tgt = tgt_ref[pl.ds(t, 1), :]
        zs_ref[pl.ds(t, 1), :] = jnp.sum(jnp.where(sub == tgt, g, 0.0), axis=0, keepdims=True)
        return carry

    lax.fori_loop(0, tb, token, 0)
    zs = zs_ref[...]
    hi = zs.astype(BF16)
    lo = (zs - hi.astype(F32)).astype(BF16)
    dn = (((1,), (1,)), ((), ()))
    z = (lax.dot_general(hi, expand, dn, preferred_element_type=F32)
         + lax.dot_general(lo, expand, dn, preferred_element_type=F32))
    a_ref[...] = _gelu_exact(z) * g_ref[...]


def _peer_v_kernel(j_sm, a_in_ref, p_ref, v_ref, o_ref, arep_ref, tgt_ref, *, tb):
    expand = _expand_matrix()
    par_rep = jnp.dot(p_ref[...].astype(BF16), expand, preferred_element_type=F32)
    tgt_ref[...] = _diag_target(par_rep)
    arep_ref[...] = jnp.dot(a_in_ref[...].astype(BF16), expand, preferred_element_type=F32)
    sub = lax.broadcasted_iota(I32, (PEER_ROWS, EXP_COLS), 0).astype(F32)

    def token(t, carry):
        tgt = tgt_ref[pl.ds(t, 1), :]
        am = jnp.where(sub == tgt, arep_ref[pl.ds(t, 1), :], 0.0).astype(BF16)
        acc = jnp.zeros((PEER_ROWS, LANES), F32)
        for c in range(N_TILES):
            w = jnp.concatenate([v_ref[j_sm[t, c * SLABS_PER_TILE + m]]
                                 for m in range(SLABS_PER_TILE)], axis=0)
            acc = acc + jnp.dot(am[:, c * 256:(c + 1) * 256], w, preferred_element_type=F32)
        o_ref[t] = acc
        return carry

    lax.fori_loop(0, tb, token, 0)


def _table_spec():
    return pl.BlockSpec((PEER_EXPERTS // 2, SLAB_ROWS, LANES), lambda i: (0, 0, 0),
                        pipeline_mode=pl.Buffered(1))


def _peer_u_call(jidx, h2, par, gate, u_slabs, tb):
    T = jidx.shape[0]
    row = pl.BlockSpec((tb, PEER_HK), lambda i: (i, 0))
    return pl.pallas_call(
        functools.partial(_peer_u_kernel, tb=tb),
        grid=(T // tb,),
        in_specs=[pl.BlockSpec((tb, PEER_HK), lambda i: (i, 0), memory_space=pltpu.SMEM),
                  pl.BlockSpec((tb, PEER_ROWS, LANES), lambda i: (i, 0, 0)),
                  row, row, _table_spec()],
        out_specs=row,
        out_shape=jax.ShapeDtypeStruct((T, PEER_HK), F32),
        scratch_shapes=[pltpu.VMEM((tb, EXP_COLS), F32), pltpu.VMEM((tb, EXP_COLS), F32)],
        compiler_params=_params(("arbitrary",)),
        name="peer_u",
    )(jidx, h2, par, gate, u_slabs)


def _peer_v_call(jidx, a, par, v_slabs, tb):
    T = jidx.shape[0]
    row = pl.BlockSpec((tb, PEER_HK), lambda i: (i, 0))
    return pl.pallas_call(
        functools.partial(_peer_v_kernel, tb=tb),
        grid=(T // tb,),
        in_specs=[pl.BlockSpec((tb, PEER_HK), lambda i: (i, 0), memory_space=pltpu.SMEM),
                  row, row, _table_spec()],
        out_specs=pl.BlockSpec((tb, PEER_ROWS, LANES), lambda i: (i, 0, 0)),
        out_shape=jax.ShapeDtypeStruct((T, PEER_ROWS, LANES), F32),
        scratch_shapes=[pltpu.VMEM((tb, EXP_COLS), F32), pltpu.VMEM((tb, EXP_COLS), F32)],
        compiler_params=_params(("arbitrary",)),
        name="peer_v",
    )(jidx, a, par, v_slabs)


def _ln2_kernel(x_ref, f_ref, g_ref, w_ref, b_ref, o_ref):
    o_ref[0] = _layer_norm(DEEP_ALPHA * x_ref[0] + (1.0 + g_ref[0]) * f_ref[0], w_ref[...], b_ref[...])


def _ln2_call(x1, ffn, g2, ln_w, ln_b, tm):
    B, S, D = x1.shape
    row = pl.BlockSpec((1, tm, D), lambda b, i: (b, i, 0))
    vec = pl.BlockSpec((1, D), lambda b, i: (0, 0))
    return pl.pallas_call(
        _ln2_kernel,
        grid=(B, S // tm),
        in_specs=[row, row, pl.BlockSpec((1, 1, D), lambda b, i: (b, 0, 0)), vec, vec],
        out_specs=row,
        out_shape=jax.ShapeDtypeStruct((B, S, D), F32),
        compiler_params=_params(("arbitrary", "arbitrary")),
        name="ln2",
    )(x1, ffn, g2, ln_w.reshape(1, -1), ln_b.reshape(1, -1))


def _tiles(B, S):
    return dict(inproj=min(S, 512), hgrn=min(S, 256), s5=min(S, 32), out=min(S, 256),
                topk=LANES, peer=min(B * S, 64), ln2=min(S, 512))


def kernel(x, c, ada_w, ada_b, w_in, hg_lower_bound, hg_norm_w, ssm_a_re, ssm_a_im, ssm_log_dt,
           ssm_b_re, ssm_b_im, ssm_c_re, ssm_c_im, ssm_d, ssm_glu_w, ssm_glu_b, ssm_norm_w,
           w_out, ln1_w, ln1_b, peer_w_q, peer_sub_keys, peer_u, peer_v, ln2_w, ln2_b):
    B, S, D = x.shape
    T = B * S
    tl = _tiles(B, S)
    lb_all = jnp.cumsum(jax.nn.softmax(hg_lower_bound.astype(F32), axis=0), axis=0)
    for l in range(DEPTH):
        mod = _mod_call(c, ada_w[l], ada_b[l]).reshape(B, 6, 1, D)
        sh1, sc1, g1, sh2, sc2, g2 = (mod[:, n] for n in range(6))

        proj = _inproj_call(x, sc1, sh1, w_in[l].astype(BF16), tl["inproj"])
        o_hg = _hgrn_call(proj, lb_all[l], hg_norm_w[l], tl["hgrn"])
        lam_re, lam_im, bm, cm = _s5_prepare(ssm_a_re[l], ssm_a_im[l], ssm_log_dt[l], ssm_b_re[l],
                                             ssm_b_im[l], ssm_c_re[l], ssm_c_im[l])
        o_ssm = _s5_call(proj, lam_re, lam_im, bm, cm, ssm_d[l], ssm_glu_w[l].astype(BF16),
                         ssm_glu_b[l], ssm_norm_w[l], tl["s5"])
        x1, h2, q2 = _out_call(x, o_hg, o_ssm, g1, sc2, sh2, w_out[l].astype(BF16), ln1_w[l], ln1_b[l],
                               peer_w_q[l].astype(BF16), tl["out"])

        keys = peer_sub_keys[l].astype(BF16).reshape(2 * PEER_HEADS, PEER_NKEYS, PEER_HALF)
        jidx, par, gate = _topk_call(q2.reshape(T, PEER_HEADS * PEER_DKEY), keys, tl["topk"])
        u_slabs = peer_u[l].astype(BF16).reshape(PEER_EXPERTS // 2, SLAB_ROWS, LANES)
        v_slabs = peer_v[l].astype(BF16).reshape(PEER_EXPERTS // 2, SLAB_ROWS, LANES)
        a = _peer_u_call(jidx, h2.reshape(T, PEER_ROWS, LANES), par, gate, u_slabs, tl["peer"])
        ffn = _peer_v_call(jidx, a, par, v_slabs, tl["peer"])
        x = _ln2_call(x1, ffn.reshape(B, S, D), g2, ln2_w[l], ln2_b[l], tl["ln2"])
    return x
```

```python
import functools
import math

import jax
import jax.numpy as jnp
from jax import lax
from jax.experimental import pallas as pl
from jax.experimental.pallas import tpu as pltpu

F32 = jnp.float32
BF16 = jnp.bfloat16
I32 = jnp.int32

D_MODEL = 1024
DEPTH = 1
D_HGRN = 512
HG_HEADS = 4
HG_DK = 128
D_SSM = 512
SSM_GROUP = 16
SSM_GROUPS = 32
SSM_STATE = 64
IN_COLS = 4 * D_HGRN + D_SSM
PEER_HEADS = 8
PEER_NKEYS = 128
PEER_EXPERTS = PEER_NKEYS * PEER_NKEYS
PEER_DKEY = 256
PEER_HALF = 128
PEER_TOPK = 16
PEER_HK = PEER_HEADS * PEER_TOPK
DEEP_ALPHA = (2.0 * DEPTH) ** 0.25
LN_EPS = 1e-5
RMS_EPS = 1e-6

LANES = 128
SUBLANES = 8
HG_CHUNK = 16
NEG_BIG = -1e30
VMEM_LIMIT = 48 * 1024 * 1024


def _sigmoid(x):
    return 1.0 / (1.0 + jnp.exp(-x))


def _silu(x):
    return x * _sigmoid(x)


def _gelu_exact(x):
    return 0.5 * x * (1.0 + lax.erf(x * (1.0 / math.sqrt(2.0))))


def _params(sem):
    return pltpu.CompilerParams(dimension_semantics=sem, vmem_limit_bytes=VMEM_LIMIT)


def _mod_kernel(c_ref, w_ref, b_ref, o_ref):
    cond = _silu(c_ref[...])
    o_ref[...] = jnp.dot(cond, w_ref[...], preferred_element_type=F32,
                         precision=lax.Precision.HIGHEST) + b_ref[...]


def _mod_call(c, ada_w, ada_b):
    B, D = c.shape
    n = ada_w.shape[1] // D
    return pl.pallas_call(
        _mod_kernel,
        grid=(n,),
        in_specs=[pl.BlockSpec((B, D), lambda j: (0, 0)),
                  pl.BlockSpec((D, D), lambda j: (0, j)),
                  pl.BlockSpec((1, D), lambda j: (0, j))],
        out_specs=pl.BlockSpec((B, D), lambda j: (0, j)),
        out_shape=jax.ShapeDtypeStruct((B, n * D), F32),
        compiler_params=_params(("arbitrary",)),
        name="mod",
    )(c, ada_w, ada_b.reshape(1, -1))


def _inproj_kernel(x_ref, sc_ref, sh_ref, w_ref, o_ref):
    h = x_ref[0] * (1.0 + sc_ref[0]) + sh_ref[0]
    o_ref[0] = jnp.dot(h.astype(BF16), w_ref[...], preferred_element_type=F32)


def _inproj_call(x, sc1, sh1, w_in_bf16, tm):
    B, S, D = x.shape
    N = w_in_bf16.shape[1]
    return pl.pallas_call(
        _inproj_kernel,
        grid=(B, S // tm),
        in_specs=[pl.BlockSpec((1, tm, D), lambda b, i: (b, i, 0)),
                  pl.BlockSpec((1, 1, D), lambda b, i: (b, 0, 0)),
                  pl.BlockSpec((1, 1, D), lambda b, i: (b, 0, 0)),
                  pl.BlockSpec((D, N), lambda b, i: (0, 0))],
        out_specs=pl.BlockSpec((1, tm, N), lambda b, i: (b, i, 0)),
        out_shape=jax.ShapeDtypeStruct((B, S, N), F32),
        compiler_params=_params(("arbitrary", "arbitrary")),
        name="inproj",
    )(x, sc1, sh1, w_in_bf16)


def _hgrn_kernel(q_ref, f_ref, i_ref, g_ref, lb_ref, nw_ref, o_ref, st_ref, b_ref, *, ts):
    @pl.when(pl.program_id(1) == 0)
    def _():
        st_ref[...] = jnp.zeros_like(st_ref)

    lb_all = lb_ref[...]
    lf = jnp.log(lb_all + (1.0 - lb_all) * _sigmoid(f_ref[0]))
    r = lax.broadcasted_iota(I32, (ts, ts), 0)
    c = lax.broadcasted_iota(I32, (ts, ts), 1)
    tri = jnp.where((r // HG_CHUNK == c // HG_CHUNK) & (c <= r), 1.0, 0.0).astype(BF16)
    hi = lf.astype(BF16)
    r1 = lf - hi.astype(F32)
    mid = r1.astype(BF16)
    lo = (r1 - mid.astype(F32)).astype(BF16)
    b_ref[...] = (jnp.dot(tri, hi, preferred_element_type=F32)
                  + jnp.dot(tri, mid, preferred_element_type=F32)
                  + jnp.dot(tri, lo, preferred_element_type=F32))

    rows = lax.broadcasted_iota(I32, (HG_CHUNK, 1), 0)

    def head_chunk(sl, h):
        hs = slice(h * HG_DK, (h + 1) * HG_DK)
        lb = lb_ref[:, hs]
        q = q_ref[0, sl, hs]
        v = i_ref[0, sl, hs]
        g = g_ref[0, sl, hs]
        qs = _silu(q)
        k = 1.0 - (lb + (1.0 - lb) * _sigmoid(f_ref[0, sl, hs]))
        b = b_ref[sl, hs]
        bl = b[HG_CHUNK - 1:HG_CHUNK, :]
        qh = qs * jnp.exp(b)
        kh = k * jnp.exp(bl - b)
        st = st_ref[h]
        o = lax.dot_general(qh.astype(BF16), st.astype(BF16), (((1,), (1,)), ((), ())),
                            preferred_element_type=F32)
        for s in range(HG_CHUNK):
            e = jnp.exp(jnp.where(rows >= s, b - b[s:s + 1, :], NEG_BIG))
            col = jnp.sum(e * qs * k[s:s + 1, :], axis=1, keepdims=True)
            o = o + col * v[s:s + 1, :]
        ms = jnp.mean(o * o, axis=1, keepdims=True)
        o_ref[0, sl, hs] = o * lax.rsqrt(ms + RMS_EPS) * nw_ref[:, hs] * _silu(g)
        upd = lax.dot_general(v.astype(BF16), kh.astype(BF16), (((0,), (0,)), ((), ())),
                              preferred_element_type=F32)
        st_ref[h] = st * jnp.exp(bl) + upd

    def chunk(ci, carry):
        sl = pl.ds(pl.multiple_of(ci * HG_CHUNK, HG_CHUNK), HG_CHUNK)
        for h in range(HG_HEADS):
            head_chunk(sl, h)
        return carry

    lax.fori_loop(0, ts // HG_CHUNK, chunk, 0)


def _hgrn_call(proj, lb, norm_w, ts):
    B, S, _ = proj.shape

    def col(j):
        return pl.BlockSpec((1, ts, D_HGRN), lambda b, i, j=j: (b, i, j))

    vec = pl.BlockSpec((1, D_HGRN), lambda b, i: (0, 0))
    return pl.pallas_call(
        functools.partial(_hgrn_kernel, ts=ts),
        grid=(B, S // ts),
        in_specs=[col(0), col(1), col(2), col(3), vec, vec],
        out_specs=pl.BlockSpec((1, ts, D_HGRN), lambda b, i: (b, i, 0)),
        out_shape=jax.ShapeDtypeStruct((B, S, D_HGRN), F32),
        scratch_shapes=[pltpu.VMEM((HG_HEADS, HG_DK, HG_DK), F32), pltpu.VMEM((ts, D_HGRN), F32)],
        compiler_params=_params(("arbitrary", "arbitrary")),
        name="hgrn",
    )(proj, proj, proj, proj, lb.reshape(1, -1), norm_w.reshape(1, -1))


S5_CB = D_SSM // LANES
S5_NS = SSM_GROUPS * SSM_STATE
S5_SB = S5_NS // S5_CB


def _s5_kernel(u_ref, lre_ref, lim_ref, bm_ref, cm_ref, d_ref, gw_ref, gb_ref, nw_ref, o_ref,
               ut_ref, bu_ref, xs_ref, y_ref, *, nb, tt):
    @pl.when(pl.program_id(0) == 0)
    def _():
        xs_ref[...] = jnp.zeros_like(xs_ref)

    for b in range(nb):
        for cb in range(S5_CB):
            ut_ref[cb, pl.ds(b, tt, stride=nb), :] = u_ref[b, :, cb * LANES:(cb + 1) * LANES]
    for cb in range(S5_CB):
        res = jnp.dot(ut_ref[cb].astype(BF16), bm_ref[cb], preferred_element_type=F32)
        bu_ref[:, cb * S5_SB:(cb + 1) * S5_SB] = res[:, :S5_SB]
        bu_ref[:, S5_NS + cb * S5_SB:S5_NS + (cb + 1) * S5_SB] = res[:, S5_SB:]

    for sb in range(S5_CB):
        re = slice(sb * S5_SB, (sb + 1) * S5_SB)
        im = slice(S5_NS + sb * S5_SB, S5_NS + (sb + 1) * S5_SB)
        lr = lre_ref[:, re]
        li = lim_ref[:, re]

        def step(t, carry, re=re, im=im, lr=lr, li=li):
            xr, xi = carry
            rows = pl.ds(pl.multiple_of(t * nb, nb), nb)
            nxr = lr * xr - li * xi + bu_ref[rows, re]
            nxi = lr * xi + li * xr + bu_ref[rows, im]
            bu_ref[rows, re] = nxr
            bu_ref[rows, im] = nxi
            return nxr, nxi

        xr, xi = lax.fori_loop(0, tt, step, (xs_ref[:, re], xs_ref[:, im]), unroll=4)
        xs_ref[:, re] = xr
        xs_ref[:, im] = xi

    ys = []
    for cb in range(S5_CB):
        xre = bu_ref[:, cb * S5_SB:(cb + 1) * S5_SB].astype(BF16)
        xim = bu_ref[:, S5_NS + cb * S5_SB:S5_NS + (cb + 1) * S5_SB].astype(BF16)
        ys.append(jnp.dot(xre, cm_ref[cb, :S5_SB, :], preferred_element_type=F32)
                  + jnp.dot(xim, cm_ref[cb, S5_SB:, :], preferred_element_type=F32)
                  + d_ref[:, cb * LANES:(cb + 1) * LANES] * ut_ref[cb])
    y = _gelu_exact(jnp.concatenate(ys, axis=1))
    z = jnp.dot(y.astype(BF16), gw_ref[...], preferred_element_type=F32) + gb_ref[...]
    y = y * _sigmoid(z)
    y = y * lax.rsqrt(jnp.mean(y * y, axis=1, keepdims=True) + RMS_EPS) * nw_ref[...]
    for cb in range(S5_CB):
        y_ref[cb] = y[:, cb * LANES:(cb + 1) * LANES]
    for b in range(nb):
        for cb in range(S5_CB):
            o_ref[b, :, cb * LANES:(cb + 1) * LANES] = y_ref[cb, pl.ds(b, tt, stride=nb), :]


def _s5_prepare(a_re, a_im, log_dt, b_re, b_im, c_re, c_im):
    ar, ai = a_re.astype(F32), a_im.astype(F32)
    dt = jnp.exp(log_dt.astype(F32))[:, None]
    mag = jnp.exp(ar * dt)
    lam_re = mag * jnp.cos(ai * dt)
    lam_im = mag * jnp.sin(ai * dt)
    den = ar * ar + ai * ai
    nr, ni = lam_re - 1.0, lam_im
    z_re = (nr * ar + ni * ai) / den
    z_im = (ni * ar - nr * ai) / den
    br, bi = b_re.astype(F32), b_im.astype(F32)
    bb_re = z_re[..., None] * br - z_im[..., None] * bi
    bb_im = z_re[..., None] * bi + z_im[..., None] * br
    gl = SSM_GROUPS // S5_CB
    eye = jnp.eye(gl, dtype=F32)

    def in_block(m):
        m = m.reshape(S5_CB, gl, SSM_STATE, SSM_GROUP)
        return jnp.einsum('cgpi,gh->cgihp', m, eye).reshape(S5_CB, gl * SSM_GROUP, gl * SSM_STATE)

    def out_block(m):
        m = m.reshape(S5_CB, gl, SSM_GROUP, SSM_STATE)
        return jnp.einsum('cgip,gh->cgphi', m, eye).reshape(S5_CB, gl * SSM_STATE, gl * SSM_GROUP)

    bm = jnp.concatenate([in_block(bb_re), in_block(bb_im)], axis=2).astype(BF16)
    cm = jnp.concatenate([out_block(c_re.astype(F32)), -out_block(c_im.astype(F32))], axis=1).astype(BF16)
    return lam_re.reshape(1, -1), lam_im.reshape(1, -1), bm, cm


def _s5_call(proj, lam_re, lam_im, bm, cm, d_skip, glu_w_bf16, glu_b, norm_w, tt):
    B, S, _ = proj.shape
    R = tt * B
    ucol = (4 * D_HGRN) // D_SSM
    full = lambda shape: pl.BlockSpec(shape, lambda i: (0,) * len(shape))
    return pl.pallas_call(
        functools.partial(_s5_kernel, nb=B, tt=tt),
        grid=(S // tt,),
        in_specs=[pl.BlockSpec((B, tt, D_SSM), lambda i: (0, i, ucol)),
                  full((1, S5_NS)), full((1, S5_NS)),
                  full((S5_CB, LANES, 2 * S5_SB)), full((S5_CB, 2 * S5_SB, LANES)),
                  full((1, D_SSM)), full((D_SSM, D_SSM)), full((1, D_SSM)), full((1, D_SSM))],
        out_specs=pl.BlockSpec((B, tt, D_SSM), lambda i: (0, i, 0)),
        out_shape=jax.ShapeDtypeStruct((B, S, D_SSM), F32),
        scratch_shapes=[pltpu.VMEM((S5_CB, R, LANES), F32), pltpu.VMEM((R, 2 * S5_NS), F32),
                        pltpu.VMEM((B, 2 * S5_NS), F32), pltpu.VMEM((S5_CB, R, LANES), F32)],
        compiler_params=_params(("arbitrary",)),
        name="s5",
    )(proj, lam_re, lam_im, bm, cm, d_skip.reshape(1, -1), glu_w_bf16, glu_b.reshape(1, -1),
      norm_w.reshape(1, -1))


def _layer_norm(y, w, b):
    mu = jnp.mean(y, axis=-1, keepdims=True)
    yc = y - mu
    var = jnp.mean(yc * yc, axis=-1, keepdims=True)
    return yc * lax.rsqrt(var + LN_EPS) * w + b


def _out_kernel(x_ref, hg_ref, ss_ref, g1_ref, sc2_ref, sh2_ref, wo_ref, lw_ref, lb_ref, wq_ref,
                x1_ref, h2_ref, q2_ref):
    mixed = (jnp.dot(hg_ref[0].astype(BF16), wo_ref[:D_HGRN, :], preferred_element_type=F32)
             + jnp.dot(ss_ref[0].astype(BF16), wo_ref[D_HGRN:, :], preferred_element_type=F32))
    x1 = _layer_norm(DEEP_ALPHA * x_ref[0] + (1.0 + g1_ref[0]) * mixed, lw_ref[...], lb_ref[...])
    x1_ref[0] = x1
    h2 = x1 * (1.0 + sc2_ref[0]) + sh2_ref[0]
    h2_ref[0] = h2
    q2_ref[0] = jnp.dot(h2.astype(BF16), wq_ref[...], preferred_element_type=F32)


def _out_call(x, o_hg, o_ssm, g1, sc2, sh2, w_out_bf16, ln_w, ln_b, w_q_bf16, tm):
    B, S, D = x.shape
    NQ = w_q_bf16.shape[1]
    row = lambda n: pl.BlockSpec((1, tm, n), lambda b, i: (b, i, 0))
    mod = pl.BlockSpec((1, 1, D), lambda b, i: (b, 0, 0))
    full = lambda shape: pl.BlockSpec(shape, lambda b, i: (0,) * len(shape))
    return pl.pallas_call(
        _out_kernel,
        grid=(B, S // tm),
        in_specs=[row(D), row(D_HGRN), row(D_SSM), mod, mod, mod,
                  full((D, D)), full((1, D)), full((1, D)), full((D, NQ))],
        out_specs=[row(D), row(D), row(NQ)],
        out_shape=[jax.ShapeDtypeStruct((B, S, D), F32), jax.ShapeDtypeStruct((B, S, D), F32),
                   jax.ShapeDtypeStruct((B, S, NQ), F32)],
        compiler_params=_params(("arbitrary", "arbitrary")),
        name="outproj",
    )(x, o_hg, o_ssm, g1, sc2, sh2, w_out_bf16, ln_w.reshape(1, -1), ln_b.reshape(1, -1), w_q_bf16)


_CAND_PIECES = ((0, 1, 0, 16, 0), (1, 1, 0, 8, 0), (2, 1, 0, 8, 0), (3, 1, 0, 8, 0),
                (8, 8, 0, 1, 0), (0, 8, 0, 1, 4), (0, 8, 1, 1, 4), (0, 8, 2, 1, 4))


def _topk_kernel(q_ref, k_ref, j_ref, p_ref, g_ref, *, tb):
    nk = PEER_NKEYS
    kiota = lax.broadcasted_iota(I32, (nk, tb), 0)
    ninf = jnp.float32(-jnp.inf)

    def top16(sc):
        vals, idxs = [], []
        for _ in range(PEER_TOPK):
            m = jnp.max(sc, axis=0, keepdims=True)
            idx = jnp.min(jnp.where(sc == m, kiota, nk), axis=0, keepdims=True)
            sc = jnp.where(kiota == idx, ninf, sc)
            vals.append(m)
            idxs.append(idx)
        return jnp.concatenate(vals, axis=0), jnp.concatenate(idxs, axis=0)

    flat_rows = []
    for a0, na, b0, nb, _ in _CAND_PIECES:
        for a in range(a0, a0 + na):
            for b in range(b0, b0 + nb):
                flat_rows.append(a * PEER_TOPK + b)
    ncand = len(flat_rows)
    riota = lax.broadcasted_iota(I32, (ncand, 1), 0)
    cflat = jnp.zeros((ncand, 1), I32)
    for rix, fv in enumerate(flat_rows):
        cflat = jnp.where(riota == rix, fv, cflat)

    e_rows, g_rows = [], []
    for h in range(PEER_HEADS):
        tops, topi = [], []
        for p in range(2):
            hp = 2 * h + p
            qhp = q_ref[:, hp * PEER_HALF:(hp + 1) * PEER_HALF].astype(BF16)
            sc = lax.dot_general(k_ref[hp], qhp, (((1,), (1,)), ((), ())),
                                 preferred_element_type=F32)
            v_, i_ = top16(sc)
            tops.append(v_)
            topi.append(i_)
        s1, s2 = tops
        i1, i2 = topi
        cv, ce = [], []
        for a0, na, b0, nb, mask_below in _CAND_PIECES:
            val = s1[a0:a0 + na, :] + s2[b0:b0 + nb, :]
            eid = i1[a0:a0 + na, :] * PEER_NKEYS + i2[b0:b0 + nb, :]
            if mask_below:
                prow = lax.broadcasted_iota(I32, (max(na, nb), 1), 0)
                val = jnp.where(prow < mask_below, ninf, val)
            cv.append(val)
            ce.append(eid)
        cand = jnp.concatenate(cv, axis=0)
        cexp = jnp.concatenate(ce, axis=0)
        bv, be = [], []
        for _ in range(PEER_TOPK):
            m = jnp.max(cand, axis=0, keepdims=True)
            cm = jnp.min(jnp.where(cand == m, cflat, PEER_TOPK * PEER_TOPK), axis=0, keepdims=True)
            sel = cflat == cm
            be.append(jnp.max(jnp.where(sel, cexp, -1), axis=0, keepdims=True))
            cand = jnp.where(sel, ninf, cand)
            bv.append(m)
        best = jnp.concatenate(bv, axis=0)
        ex = jnp.exp(best - best[0:1, :])
        g_rows.append(ex / jnp.sum(ex, axis=0, keepdims=True))
        e_rows.append(jnp.concatenate(be, axis=0))
    expert = jnp.concatenate(e_rows, axis=0)
    gate = jnp.concatenate(g_rows, axis=0)
    j_ref[...] = jnp.transpose(((expert >> 1) * PEER_ROWS).astype(F32)).astype(I32)
    p_ref[...] = jnp.transpose((expert & 1).astype(F32))
    g_ref[...] = jnp.transpose(gate)


def _topk_call(q2, keys_bf16, tb):
    T = q2.shape[0]
    out = lambda dt: jax.ShapeDtypeStruct((T, PEER_HK), dt)
    ospec = pl.BlockSpec((tb, PEER_HK), lambda i: (i, 0))
    return pl.pallas_call(
        functools.partial(_topk_kernel, tb=tb),
        grid=(T // tb,),
        in_specs=[pl.BlockSpec((tb, PEER_HEADS * PEER_DKEY), lambda i: (i, 0)),
                  pl.BlockSpec((2 * PEER_HEADS, PEER_NKEYS, PEER_HALF), lambda i: (0, 0, 0))],
        out_specs=[ospec, ospec, ospec],
        out_shape=[out(I32), out(F32), out(F32)],
        compiler_params=_params(("arbitrary",)),
        name="topk",
    )(q2, keys_bf16)


PEER_ROWS = D_MODEL // LANES
SLAB_ROWS = 2 * PEER_ROWS
SLABS_PER_TILE = 16
N_TILES = PEER_HK // SLABS_PER_TILE
EXP_COLS = PEER_HK * SLAB_ROWS
TOKEN_UNROLL = 8


def _pack_slabs(table):
    t = table.astype(BF16).reshape(PEER_EXPERTS // 2, PEER_ROWS, 2, LANES)
    words = lax.bitcast_convert_type(jnp.swapaxes(t, 2, 3), jnp.uint32)
    return words.reshape(PEER_EXPERTS // 2 * PEER_ROWS, LANES)


def _load_tile(tab_ref, j_sm, t, c):
    slabs = []
    for m in range(SLABS_PER_TILE):
        off = pl.multiple_of(j_sm[t, c * SLABS_PER_TILE + m], PEER_ROWS)
        slabs.append(pltpu.bitcast(tab_ref[pl.ds(off, PEER_ROWS), :], BF16))
    return jnp.concatenate(slabs, axis=0)


def _expand_matrix():
    r = lax.broadcasted_iota(I32, (PEER_HK, EXP_COLS), 0)
    c = lax.broadcasted_iota(I32, (PEER_HK, EXP_COLS), 1)
    return jnp.where(c // SLAB_ROWS == r, 1.0, 0.0).astype(BF16)


def _diag_target(par_rep):
    lane = lax.broadcasted_iota(I32, par_rep.shape, 1)
    return (lane % SLAB_ROWS).astype(F32) - PEER_ROWS * par_rep


def _peer_u_kernel(j_sm, h_ref, p_ref, g_ref, u_ref, a_ref, tgt_ref, zs_ref, *, tb):
    expand = _expand_matrix()
    par_rep = jnp.dot(p_ref[...].astype(BF16), expand, preferred_element_type=F32)
    tgt_ref[...] = _diag_target(par_rep)
    sub = lax.broadcasted_iota(I32, (PEER_ROWS, EXP_COLS), 0).astype(F32)

    def token(t):
        hb = h_ref[t].astype(BF16)
        gs = []
        for c in range(N_TILES):
            w = _load_tile(u_ref, j_sm, t, c)
            gs.append(lax.dot_general(hb, w, (((1,), (1,)), ((), ())),
                                      preferred_element_type=F32))
        g = jnp.concatenate(gs, axis=1)
        tgt = tgt_ref[pl.ds(t, 1), :]
        zs_ref[pl.ds(t, 1), :] = jnp.sum(jnp.where(sub == tgt, g, 0.0), axis=0, keepdims=True)

    def tokens(i, carry):
        for k in range(TOKEN_UNROLL):
            token(i * TOKEN_UNROLL + k)
        return carry

    lax.fori_loop(0, tb // TOKEN_UNROLL, tokens, 0)
    zs = zs_ref[...]
    hi = zs.astype(BF16)
    lo = (zs - hi.astype(F32)).astype(BF16)
    dn = (((1,), (1,)), ((), ()))
    z = (lax.dot_general(hi, expand, dn, preferred_element_type=F32)
         + lax.dot_general(lo, expand, dn, preferred_element_type=F32))
    a_ref[...] = _gelu_exact(z) * g_ref[...]


def _peer_v_kernel(j_sm, a_in_ref, p_ref, v_ref, o_ref, arep_ref, tgt_ref, *, tb):
    expand = _expand_matrix()
    par_rep = jnp.dot(p_ref[...].astype(BF16), expand, preferred_element_type=F32)
    tgt_ref[...] = _diag_target(par_rep)
    arep_ref[...] = jnp.dot(a_in_ref[...].astype(BF16), expand, preferred_element_type=F32)
    sub = lax.broadcasted_iota(I32, (PEER_ROWS, EXP_COLS), 0).astype(F32)

    def token(t):
        tgt = tgt_ref[pl.ds(t, 1), :]
        am = jnp.where(sub == tgt, arep_ref[pl.ds(t, 1), :], 0.0).astype(BF16)
        acc = jnp.zeros((PEER_ROWS, LANES), F32)
        for c in range(N_TILES):
            w = _load_tile(v_ref, j_sm, t, c)
            acc = acc + jnp.dot(am[:, c * 256:(c + 1) * 256], w, preferred_element_type=F32)
        o_ref[t] = acc

    def tokens(i, carry):
        for k in range(TOKEN_UNROLL):
            token(i * TOKEN_UNROLL + k)
        return carry

    lax.fori_loop(0, tb // TOKEN_UNROLL, tokens, 0)


def _table_spec():
    return pl.BlockSpec((PEER_EXPERTS // 2 * PEER_ROWS, LANES), lambda i: (0, 0),
                        pipeline_mode=pl.Buffered(1))


def _peer_u_call(jidx, h2, par, gate, u_slabs, tb):
    T = jidx.shape[0]
    row = pl.BlockSpec((tb, PEER_HK), lambda i: (i, 0))
    return pl.pallas_call(
        functools.partial(_peer_u_kernel, tb=tb),
        grid=(T // tb,),
        in_specs=[pl.BlockSpec((tb, PEER_HK), lambda i: (i, 0), memory_space=pltpu.SMEM),
                  pl.BlockSpec((tb, PEER_ROWS, LANES), lambda i: (i, 0, 0)),
                  row, row, _table_spec()],
        out_specs=row,
        out_shape=jax.ShapeDtypeStruct((T, PEER_HK), F32),
        scratch_shapes=[pltpu.VMEM((tb, EXP_COLS), F32), pltpu.VMEM((tb, EXP_COLS), F32)],
        compiler_params=_params(("arbitrary",)),
        name="peer_u",
    )(jidx, h2, par, gate, u_slabs)


def _peer_v_call(jidx, a, par, v_slabs, tb):
    T = jidx.shape[0]
    row = pl.BlockSpec((tb, PEER_HK), lambda i: (i, 0))
    return pl.pallas_call(
        functools.partial(_peer_v_kernel, tb=tb),
        grid=(T // tb,),
        in_specs=[pl.BlockSpec((tb, PEER_HK), lambda i: (i, 0), memory_space=pltpu.SMEM),
                  row, row, _table_spec()],
        out_specs=pl.BlockSpec((tb, PEER_ROWS, LANES), lambda i: (i, 0, 0)),
        out_shape=jax.ShapeDtypeStruct((T, PEER_ROWS, LANES), F32),
        scratch_shapes=[pltpu.VMEM((tb, EXP_COLS), F32), pltpu.VMEM((tb, EXP_COLS), F32)],
        compiler_params=_params(("arbitrary",)),
        name="peer_v",
    )(jidx, a, par, v_slabs)


def _ln2_kernel(x_ref, f_ref, g_ref, w_ref, b_ref, o_ref):
    o_ref[0] = _layer_norm(DEEP_ALPHA * x_ref[0] + (1.0 + g_ref[0]) * f_ref[0], w_ref[...], b_ref[...])


def _ln2_call(x1, ffn, g2, ln_w, ln_b, tm):
    B, S, D = x1.shape
    row = pl.BlockSpec((1, tm, D), lambda b, i: (b, i, 0))
    vec = pl.BlockSpec((1, D), lambda b, i: (0, 0))
    return pl.pallas_call(
        _ln2_kernel,
        grid=(B, S // tm),
        in_specs=[row, row, pl.BlockSpec((1, 1, D), lambda b, i: (b, 0, 0)), vec, vec],
        out_specs=row,
        out_shape=jax.ShapeDtypeStruct((B, S, D), F32),
        compiler_params=_params(("arbitrary", "arbitrary")),
        name="ln2",
    )(x1, ffn, g2, ln_w.reshape(1, -1), ln_b.reshape(1, -1))


def _tiles(B, S):
    return dict(inproj=min(S, 512), hgrn=min(S, 256), s5=min(S, 32), out=min(S, 256),
                topk=LANES, peer=min(B * S, 64), ln2=min(S, 512))


def kernel(x, c, ada_w, ada_b, w_in, hg_lower_bound, hg_norm_w, ssm_a_re, ssm_a_im, ssm_log_dt,
           ssm_b_re, ssm_b_im, ssm_c_re, ssm_c_im, ssm_d, ssm_glu_w, ssm_glu_b, ssm_norm_w,
           w_out, ln1_w, ln1_b, peer_w_q, peer_sub_keys, peer_u, peer_v, ln2_w, ln2_b):
    B, S, D = x.shape
    T = B * S
    tl = _tiles(B, S)
    lb_all = jnp.cumsum(jax.nn.softmax(hg_lower_bound.astype(F32), axis=0), axis=0)
    for l in range(DEPTH):
        mod = _mod_call(c, ada_w[l], ada_b[l]).reshape(B, 6, 1, D)
        sh1, sc1, g1, sh2, sc2, g2 = (mod[:, n] for n in range(6))

        proj = _inproj_call(x, sc1, sh1, w_in[l].astype(BF16), tl["inproj"])
        o_hg = _hgrn_call(proj, lb_all[l], hg_norm_w[l], tl["hgrn"])
        lam_re, lam_im, bm, cm = _s5_prepare(ssm_a_re[l], ssm_a_im[l], ssm_log_dt[l], ssm_b_re[l],
                                             ssm_b_im[l], ssm_c_re[l], ssm_c_im[l])
        o_ssm = _s5_call(proj, lam_re, lam_im, bm, cm, ssm_d[l], ssm_glu_w[l].astype(BF16),
                         ssm_glu_b[l], ssm_norm_w[l], tl["s5"])
        x1, h2, q2 = _out_call(x, o_hg, o_ssm, g1, sc2, sh2, w_out[l].astype(BF16), ln1_w[l], ln1_b[l],
                               peer_w_q[l].astype(BF16), tl["out"])

        keys = peer_sub_keys[l].astype(BF16).reshape(2 * PEER_HEADS, PEER_NKEYS, PEER_HALF)
        jidx, par, gate = _topk_call(q2.reshape(T, PEER_HEADS * PEER_DKEY), keys, tl["topk"])
        u_slabs = _pack_slabs(peer_u[l])
        v_slabs = _pack_slabs(peer_v[l])
        a = _peer_u_call(jidx, h2.reshape(T, PEER_ROWS, LANES), par, gate, u_slabs, tl["peer"])
        ffn = _peer_v_call(jidx, a, par, v_slabs, tl["peer"])
        x = _ln2_call(x1, ffn.reshape(B, S, D), g2, ln2_w[l], ln2_b[l], tl["ln2"])
    return x
```

```python
import functools
import math

import jax
import jax.numpy as jnp
from jax import lax
from jax.experimental import pallas as pl
from jax.experimental.pallas import tpu as pltpu

F32 = jnp.float32
BF16 = jnp.bfloat16
I32 = jnp.int32

D_MODEL = 1024
DEPTH = 1
D_HGRN = 512
HG_HEADS = 4
HG_DK = 128
D_SSM = 512
SSM_GROUP = 16
SSM_GROUPS = 32
SSM_STATE = 64
IN_COLS = 4 * D_HGRN + D_SSM
PEER_HEADS = 8
PEER_NKEYS = 128
PEER_EXPERTS = PEER_NKEYS * PEER_NKEYS
PEER_DKEY = 256
PEER_HALF = 128
PEER_TOPK = 16
PEER_HK = PEER_HEADS * PEER_TOPK
DEEP_ALPHA = (2.0 * DEPTH) ** 0.25
LN_EPS = 1e-5
RMS_EPS = 1e-6

LANES = 128
SUBLANES = 8
HG_CHUNK = 16
NEG_BIG = -1e30
VMEM_LIMIT = 48 * 1024 * 1024


def _sigmoid(x):
    return 1.0 / (1.0 + jnp.exp(-x))


def _silu(x):
    return x * _sigmoid(x)


def _gelu_exact(x):
    return 0.5 * x * (1.0 + lax.erf(x * (1.0 / math.sqrt(2.0))))


def _params(sem):
    return pltpu.CompilerParams(dimension_semantics=sem, vmem_limit_bytes=VMEM_LIMIT)


def _mod_kernel(c_ref, w_ref, b_ref, o_ref):
    cond = _silu(c_ref[...])
    o_ref[...] = jnp.dot(cond, w_ref[...], preferred_element_type=F32,
                         precision=lax.Precision.HIGHEST) + b_ref[...]


def _mod_call(c, ada_w, ada_b):
    B, D = c.shape
    n = ada_w.shape[1] // D
    return pl.pallas_call(
        _mod_kernel,
        grid=(n,),
        in_specs=[pl.BlockSpec((B, D), lambda j: (0, 0)),
                  pl.BlockSpec((D, D), lambda j: (0, j)),
                  pl.BlockSpec((1, D), lambda j: (0, j))],
        out_specs=pl.BlockSpec((B, D), lambda j: (0, j)),
        out_shape=jax.ShapeDtypeStruct((B, n * D), F32),
        compiler_params=_params(("arbitrary",)),
        name="mod",
    )(c, ada_w, ada_b.reshape(1, -1))


def _inproj_kernel(x_ref, sc_ref, sh_ref, w_ref, o_ref):
    h = x_ref[0] * (1.0 + sc_ref[0]) + sh_ref[0]
    o_ref[0] = jnp.dot(h.astype(BF16), w_ref[...], preferred_element_type=F32)


def _inproj_call(x, sc1, sh1, w_in_bf16, tm):
    B, S, D = x.shape
    N = w_in_bf16.shape[1]
    return pl.pallas_call(
        _inproj_kernel,
        grid=(B, S // tm),
        in_specs=[pl.BlockSpec((1, tm, D), lambda b, i: (b, i, 0)),
                  pl.BlockSpec((1, 1, D), lambda b, i: (b, 0, 0)),
                  pl.BlockSpec((1, 1, D), lambda b, i: (b, 0, 0)),
                  pl.BlockSpec((D, N), lambda b, i: (0, 0))],
        out_specs=pl.BlockSpec((1, tm, N), lambda b, i: (b, i, 0)),
        out_shape=jax.ShapeDtypeStruct((B, S, N), F32),
        compiler_params=_params(("arbitrary", "arbitrary")),
        name="inproj",
    )(x, sc1, sh1, w_in_bf16)


def _hgrn_kernel(q_ref, f_ref, i_ref, g_ref, lb_ref, nw_ref, o_ref, st_ref, b_ref, *, ts):
    @pl.when(pl.program_id(1) == 0)
    def _():
        st_ref[...] = jnp.zeros_like(st_ref)

    lb_all = lb_ref[...]
    lf = jnp.log(lb_all + (1.0 - lb_all) * _sigmoid(f_ref[0]))
    r = lax.broadcasted_iota(I32, (ts, ts), 0)
    c = lax.broadcasted_iota(I32, (ts, ts), 1)
    tri = jnp.where((r // HG_CHUNK == c // HG_CHUNK) & (c <= r), 1.0, 0.0).astype(BF16)
    hi = lf.astype(BF16)
    r1 = lf - hi.astype(F32)
    mid = r1.astype(BF16)
    lo = (r1 - mid.astype(F32)).astype(BF16)
    b_ref[...] = (jnp.dot(tri, hi, preferred_element_type=F32)
                  + jnp.dot(tri, mid, preferred_element_type=F32)
                  + jnp.dot(tri, lo, preferred_element_type=F32))

    rows = lax.broadcasted_iota(I32, (HG_CHUNK, 1), 0)

    def head_chunk(sl, h):
        hs = slice(h * HG_DK, (h + 1) * HG_DK)
        lb = lb_ref[:, hs]
        q = q_ref[0, sl, hs]
        v = i_ref[0, sl, hs]
        g = g_ref[0, sl, hs]
        qs = _silu(q)
        k = 1.0 - (lb + (1.0 - lb) * _sigmoid(f_ref[0, sl, hs]))
        b = b_ref[sl, hs]
        bl = b[HG_CHUNK - 1:HG_CHUNK, :]
        qh = qs * jnp.exp(b)
        kh = k * jnp.exp(bl - b)
        st = st_ref[h]
        o = lax.dot_general(qh.astype(BF16), st.astype(BF16), (((1,), (1,)), ((), ())),
                            preferred_element_type=F32)
        for s in range(HG_CHUNK):
            e = jnp.exp(jnp.where(rows >= s, b - b[s:s + 1, :], NEG_BIG))
            col = jnp.sum(e * qs * k[s:s + 1, :], axis=1, keepdims=True)
            o = o + col * v[s:s + 1, :]
        ms = jnp.mean(o * o, axis=1, keepdims=True)
        o_ref[0, sl, hs] = o * lax.rsqrt(ms + RMS_EPS) * nw_ref[:, hs] * _silu(g)
        upd = lax.dot_general(v.astype(BF16), kh.astype(BF16), (((0,), (0,)), ((), ())),
                              preferred_element_type=F32)
        st_ref[h] = st * jnp.exp(bl) + upd

    def chunk(ci, carry):
        sl = pl.ds(pl.multiple_of(ci * HG_CHUNK, HG_CHUNK), HG_CHUNK)
        for h in range(HG_HEADS):
            head_chunk(sl, h)
        return carry

    lax.fori_loop(0, ts // HG_CHUNK, chunk, 0)


def _hgrn_call(proj, lb, norm_w, ts):
    B, S, _ = proj.shape

    def col(j):
        return pl.BlockSpec((1, ts, D_HGRN), lambda b, i, j=j: (b, i, j))

    vec = pl.BlockSpec((1, D_HGRN), lambda b, i: (0, 0))
    return pl.pallas_call(
        functools.partial(_hgrn_kernel, ts=ts),
        grid=(B, S // ts),
        in_specs=[col(0), col(1), col(2), col(3), vec, vec],
        out_specs=pl.BlockSpec((1, ts, D_HGRN), lambda b, i: (b, i, 0)),
        out_shape=jax.ShapeDtypeStruct((B, S, D_HGRN), F32),
        scratch_shapes=[pltpu.VMEM((HG_HEADS, HG_DK, HG_DK), F32), pltpu.VMEM((ts, D_HGRN), F32)],
        compiler_params=_params(("arbitrary", "arbitrary")),
        name="hgrn",
    )(proj, proj, proj, proj, lb.reshape(1, -1), norm_w.reshape(1, -1))


S5_CB = D_SSM // LANES
S5_NS = SSM_GROUPS * SSM_STATE
S5_SB = S5_NS // S5_CB


def _s5_kernel(u_ref, lre_ref, lim_ref, bm_ref, cm_ref, d_ref, gw_ref, gb_ref, nw_ref, o_ref,
               ut_ref, bu_ref, xs_ref, y_ref, *, nb, tt):
    @pl.when(pl.program_id(0) == 0)
    def _():
        xs_ref[...] = jnp.zeros_like(xs_ref)

    for b in range(nb):
        for cb in range(S5_CB):
            ut_ref[cb, pl.ds(b, tt, stride=nb), :] = u_ref[b, :, cb * LANES:(cb + 1) * LANES]
    for cb in range(S5_CB):
        res = jnp.dot(ut_ref[cb].astype(BF16), bm_ref[cb], preferred_element_type=F32)
        bu_ref[:, cb * S5_SB:(cb + 1) * S5_SB] = res[:, :S5_SB]
        bu_ref[:, S5_NS + cb * S5_SB:S5_NS + (cb + 1) * S5_SB] = res[:, S5_SB:]

    for sb in range(S5_CB):
        re = slice(sb * S5_SB, (sb + 1) * S5_SB)
        im = slice(S5_NS + sb * S5_SB, S5_NS + (sb + 1) * S5_SB)
        lr = lre_ref[:, re]
        li = lim_ref[:, re]

        def step(t, carry, re=re, im=im, lr=lr, li=li):
            xr, xi = carry
            rows = pl.ds(pl.multiple_of(t * nb, nb), nb)
            nxr = lr * xr - li * xi + bu_ref[rows, re]
            nxi = lr * xi + li * xr + bu_ref[rows, im]
            bu_ref[rows, re] = nxr
            bu_ref[rows, im] = nxi
            return nxr, nxi

        xr, xi = lax.fori_loop(0, tt, step, (xs_ref[:, re], xs_ref[:, im]), unroll=4)
        xs_ref[:, re] = xr
        xs_ref[:, im] = xi

    ys = []
    for cb in range(S5_CB):
        xre = bu_ref[:, cb * S5_SB:(cb + 1) * S5_SB].astype(BF16)
        xim = bu_ref[:, S5_NS + cb * S5_SB:S5_NS + (cb + 1) * S5_SB].astype(BF16)
        ys.append(jnp.dot(xre, cm_ref[cb, :S5_SB, :], preferred_element_type=F32)
                  + jnp.dot(xim, cm_ref[cb, S5_SB:, :], preferred_element_type=F32)
                  + d_ref[:, cb * LANES:(cb + 1) * LANES] * ut_ref[cb])
    y = _gelu_exact(jnp.concatenate(ys, axis=1))
    z = jnp.dot(y.astype(BF16), gw_ref[...], preferred_element_type=F32) + gb_ref[...]
    y = y * _sigmoid(z)
    y = y * lax.rsqrt(jnp.mean(y * y, axis=1, keepdims=True) + RMS_EPS) * nw_ref[...]
    for cb in range(S5_CB):
        y_ref[cb] = y[:, cb * LANES:(cb + 1) * LANES]
    for b in range(nb):
        for cb in range(S5_CB):
            o_ref[b, :, cb * LANES:(cb + 1) * LANES] = y_ref[cb, pl.ds(b, tt, stride=nb), :]


def _s5_prepare(a_re, a_im, log_dt, b_re, b_im, c_re, c_im):
    ar, ai = a_re.astype(F32), a_im.astype(F32)
    dt = jnp.exp(log_dt.astype(F32))[:, None]
    mag = jnp.exp(ar * dt)
    lam_re = mag * jnp.cos(ai * dt)
    lam_im = mag * jnp.sin(ai * dt)
    den = ar * ar + ai * ai
    nr, ni = lam_re - 1.0, lam_im
    z_re = (nr * ar + ni * ai) / den
    z_im = (ni * ar - nr * ai) / den
    br, bi = b_re.astype(F32), b_im.astype(F32)
    bb_re = z_re[..., None] * br - z_im[..., None] * bi
    bb_im = z_re[..., None] * bi + z_im[..., None] * br
    gl = SSM_GROUPS // S5_CB
    eye = jnp.eye(gl, dtype=F32)

    def in_block(m):
        m = m.reshape(S5_CB, gl, SSM_STATE, SSM_GROUP)
        return jnp.einsum('cgpi,gh->cgihp', m, eye).reshape(S5_CB, gl * SSM_GROUP, gl * SSM_STATE)

    def out_block(m):
        m = m.reshape(S5_CB, gl, SSM_GROUP, SSM_STATE)
        return jnp.einsum('cgip,gh->cgphi', m, eye).reshape(S5_CB, gl * SSM_STATE, gl * SSM_GROUP)

    bm = jnp.concatenate([in_block(bb_re), in_block(bb_im)], axis=2).astype(BF16)
    cm = jnp.concatenate([out_block(c_re.astype(F32)), -out_block(c_im.astype(F32))], axis=1).astype(BF16)
    return lam_re.reshape(1, -1), lam_im.reshape(1, -1), bm, cm


def _s5_call(proj, lam_re, lam_im, bm, cm, d_skip, glu_w_bf16, glu_b, norm_w, tt):
    B, S, _ = proj.shape
    R = tt * B
    ucol = (4 * D_HGRN) // D_SSM
    full = lambda shape: pl.BlockSpec(shape, lambda i: (0,) * len(shape))
    return pl.pallas_call(
        functools.partial(_s5_kernel, nb=B, tt=tt),
        grid=(S // tt,),
        in_specs=[pl.BlockSpec((B, tt, D_SSM), lambda i: (0, i, ucol)),
                  full((1, S5_NS)), full((1, S5_NS)),
                  full((S5_CB, LANES, 2 * S5_SB)), full((S5_CB, 2 * S5_SB, LANES)),
                  full((1, D_SSM)), full((D_SSM, D_SSM)), full((1, D_SSM)), full((1, D_SSM))],
        out_specs=pl.BlockSpec((B, tt, D_SSM), lambda i: (0, i, 0)),
        out_shape=jax.ShapeDtypeStruct((B, S, D_SSM), F32),
        scratch_shapes=[pltpu.VMEM((S5_CB, R, LANES), F32), pltpu.VMEM((R, 2 * S5_NS), F32),
                        pltpu.VMEM((B, 2 * S5_NS), F32), pltpu.VMEM((S5_CB, R, LANES), F32)],
        compiler_params=_params(("arbitrary",)),
        name="s5",
    )(proj, lam_re, lam_im, bm, cm, d_skip.reshape(1, -1), glu_w_bf16, glu_b.reshape(1, -1),
      norm_w.reshape(1, -1))


def _layer_norm(y, w, b):
    mu = jnp.mean(y, axis=-1, keepdims=True)
    yc = y - mu
    var = jnp.mean(yc * yc, axis=-1, keepdims=True)
    return yc * lax.rsqrt(var + LN_EPS) * w + b


def _out_kernel(x_ref, hg_ref, ss_ref, g1_ref, sc2_ref, sh2_ref, wo_ref, lw_ref, lb_ref, wq_ref,
                x1_ref, h2_ref, q2_ref):
    mixed = (jnp.dot(hg_ref[0].astype(BF16), wo_ref[:D_HGRN, :], preferred_element_type=F32)
             + jnp.dot(ss_ref[0].astype(BF16), wo_ref[D_HGRN:, :], preferred_element_type=F32))
    x1 = _layer_norm(DEEP_ALPHA * x_ref[0] + (1.0 + g1_ref[0]) * mixed, lw_ref[...], lb_ref[...])
    x1_ref[0] = x1
    h2 = x1 * (1.0 + sc2_ref[0]) + sh2_ref[0]
    h2_ref[0] = h2
    q2_ref[0] = jnp.dot(h2.astype(BF16), wq_ref[...], preferred_element_type=F32)


def _out_call(x, o_hg, o_ssm, g1, sc2, sh2, w_out_bf16, ln_w, ln_b, w_q_bf16, tm):
    B, S, D = x.shape
    NQ = w_q_bf16.shape[1]
    row = lambda n: pl.BlockSpec((1, tm, n), lambda b, i: (b, i, 0))
    mod = pl.BlockSpec((1, 1, D), lambda b, i: (b, 0, 0))
    full = lambda shape: pl.BlockSpec(shape, lambda b, i: (0,) * len(shape))
    return pl.pallas_call(
        _out_kernel,
        grid=(B, S // tm),
        in_specs=[row(D), row(D_HGRN), row(D_SSM), mod, mod, mod,
                  full((D, D)), full((1, D)), full((1, D)), full((D, NQ))],
        out_specs=[row(D), row(D), row(NQ)],
        out_shape=[jax.ShapeDtypeStruct((B, S, D), F32), jax.ShapeDtypeStruct((B, S, D), F32),
                   jax.ShapeDtypeStruct((B, S, NQ), F32)],
        compiler_params=_params(("arbitrary", "arbitrary")),
        name="outproj",
    )(x, o_hg, o_ssm, g1, sc2, sh2, w_out_bf16, ln_w.reshape(1, -1), ln_b.reshape(1, -1), w_q_bf16)


_CAND_PIECES = ((0, 1, 0, 16, 0), (1, 1, 0, 8, 0), (2, 1, 0, 8, 0), (3, 1, 0, 8, 0),
                (8, 8, 0, 1, 0), (0, 8, 0, 1, 4), (0, 8, 1, 1, 4), (0, 8, 2, 1, 4))


def _topk_kernel(q_ref, k_ref, j_ref, p_ref, g_ref, *, tb):
    nk = PEER_NKEYS
    kiota = lax.broadcasted_iota(I32, (nk, tb), 0)
    ninf = jnp.float32(-jnp.inf)

    def top16(sc):
        vals, idxs = [], []
        for _ in range(PEER_TOPK):
            m = jnp.max(sc, axis=0, keepdims=True)
            idx = jnp.min(jnp.where(sc == m, kiota, nk), axis=0, keepdims=True)
            sc = jnp.where(kiota == idx, ninf, sc)
            vals.append(m)
            idxs.append(idx)
        return jnp.concatenate(vals, axis=0), jnp.concatenate(idxs, axis=0)

    flat_rows = []
    for a0, na, b0, nb, _ in _CAND_PIECES:
        for a in range(a0, a0 + na):
            for b in range(b0, b0 + nb):
                flat_rows.append(a * PEER_TOPK + b)
    ncand = len(flat_rows)
    riota = lax.broadcasted_iota(I32, (ncand, 1), 0)
    cflat = jnp.zeros((ncand, 1), I32)
    for rix, fv in enumerate(flat_rows):
        cflat = jnp.where(riota == rix, fv, cflat)

    e_rows, g_rows = [], []
    for h in range(PEER_HEADS):
        tops, topi = [], []
        for p in range(2):
            hp = 2 * h + p
            qhp = q_ref[:, hp * PEER_HALF:(hp + 1) * PEER_HALF].astype(BF16)
            sc = lax.dot_general(k_ref[hp], qhp, (((1,), (1,)), ((), ())),
                                 preferred_element_type=F32)
            v_, i_ = top16(sc)
            tops.append(v_)
            topi.append(i_)
        s1, s2 = tops
        i1, i2 = topi
        cv, ce = [], []
        for a0, na, b0, nb, mask_below in _CAND_PIECES:
            val = s1[a0:a0 + na, :] + s2[b0:b0 + nb, :]
            eid = i1[a0:a0 + na, :] * PEER_NKEYS + i2[b0:b0 + nb, :]
            if mask_below:
                prow = lax.broadcasted_iota(I32, (max(na, nb), 1), 0)
                val = jnp.where(prow < mask_below, ninf, val)
            cv.append(val)
            ce.append(eid)
        cand = jnp.concatenate(cv, axis=0)
        cexp = jnp.concatenate(ce, axis=0)
        bv, be = [], []
        for _ in range(PEER_TOPK):
            m = jnp.max(cand, axis=0, keepdims=True)
            cm = jnp.min(jnp.where(cand == m, cflat, PEER_TOPK * PEER_TOPK), axis=0, keepdims=True)
            sel = cflat == cm
            be.append(jnp.max(jnp.where(sel, cexp, -1), axis=0, keepdims=True))
            cand = jnp.where(sel, ninf, cand)
            bv.append(m)
        best = jnp.concatenate(bv, axis=0)
        ex = jnp.exp(best - best[0:1, :])
        g_rows.append(ex / jnp.sum(ex, axis=0, keepdims=True))
        e_rows.append(jnp.concatenate(be, axis=0))
    expert = jnp.concatenate(e_rows, axis=0)
    gate = jnp.concatenate(g_rows, axis=0)
    j_ref[...] = jnp.transpose(((expert >> 1) * PEER_ROWS).astype(F32)).astype(I32)
    p_ref[...] = jnp.transpose((expert & 1).astype(F32))
    g_ref[...] = jnp.transpose(gate)


def _topk_call(q2, keys_bf16, tb):
    T = q2.shape[0]
    out = lambda dt: jax.ShapeDtypeStruct((T, PEER_HK), dt)
    ospec = pl.BlockSpec((tb, PEER_HK), lambda i: (i, 0))
    return pl.pallas_call(
        functools.partial(_topk_kernel, tb=tb),
        grid=(T // tb,),
        in_specs=[pl.BlockSpec((tb, PEER_HEADS * PEER_DKEY), lambda i: (i, 0)),
                  pl.BlockSpec((2 * PEER_HEADS, PEER_NKEYS, PEER_HALF), lambda i: (0, 0, 0))],
        out_specs=[ospec, ospec, ospec],
        out_shape=[out(I32), out(F32), out(F32)],
        compiler_params=_params(("arbitrary",)),
        name="topk",
    )(q2, keys_bf16)


PEER_ROWS = D_MODEL // LANES
SLAB_ROWS = 2 * PEER_ROWS
SLABS_PER_TILE = 16
N_TILES = PEER_HK // SLABS_PER_TILE
EXP_COLS = PEER_HK * SLAB_ROWS
TOKEN_UNROLL = 8


def _pack_slabs(table):
    t = table.reshape(PEER_EXPERTS // 2, 2, D_MODEL)
    bits = lambda x: lax.bitcast_convert_type(x.astype(BF16).astype(F32), jnp.uint32)
    words = (bits(t[:, 0]) >> 16) | (bits(t[:, 1]) & jnp.uint32(0xFFFF0000))
    return words.reshape(PEER_EXPERTS // 2 * PEER_ROWS, LANES)


def _load_tile(tab_ref, j_sm, t, c, scale=None):
    slabs = []
    for m in range(SLABS_PER_TILE):
        off = pl.multiple_of(j_sm[t, c * SLABS_PER_TILE + m], PEER_ROWS)
        slab = pltpu.bitcast(tab_ref[pl.ds(off, PEER_ROWS), :], BF16)
        slabs.append(slab if scale is None else slab * scale)
    return jnp.concatenate(slabs, axis=0)


def _expand_matrix():
    r = lax.broadcasted_iota(I32, (PEER_HK, EXP_COLS), 0)
    c = lax.broadcasted_iota(I32, (PEER_HK, EXP_COLS), 1)
    return jnp.where(c // SLAB_ROWS == r, 1.0, 0.0).astype(BF16)


def _diag_target(par_rep):
    q = lax.broadcasted_iota(I32, par_rep.shape, 1) % SLAB_ROWS
    return jnp.where((q % 2).astype(F32) == par_rep, (q // 2).astype(F32), -1.0)


def _peer_u_kernel(j_sm, h_ref, p_ref, g_ref, u_ref, a_ref, sel_ref, part_ref, zs_ref, *, tb):
    expand = _expand_matrix()
    par_rep = jnp.dot(p_ref[...].astype(BF16), expand, preferred_element_type=F32)
    q = lax.broadcasted_iota(I32, (tb, EXP_COLS), 1) % SLAB_ROWS
    sel_ref[...] = jnp.where((q % 2).astype(F32) == par_rep, 1.0, 0.0)
    tile_cols = SLABS_PER_TILE * SLAB_ROWS
    slot_of_col = lax.broadcasted_iota(I32, (SLABS_PER_TILE, tile_cols), 1) // SLAB_ROWS
    own_cols = slot_of_col == lax.broadcasted_iota(I32, (SLABS_PER_TILE, tile_cols), 0)
    part_ref[...] = jnp.zeros_like(part_ref)

    def finish(row0):
        for k in range(TOKEN_UNROLL):
            part = part_ref[k * PEER_HK:(k + 1) * PEER_HK, :]
            zs_ref[pl.ds(row0 + k, 1), :] = jnp.sum(jnp.transpose(part), axis=0, keepdims=True)

    def token(t, k):
        hbits = lax.bitcast_convert_type(h_ref[t].astype(BF16).astype(F32), jnp.uint32)
        hh = pltpu.bitcast((hbits >> 16) | hbits, BF16)
        sel = sel_ref[pl.ds(t, 1), :]
        for c in range(N_TILES):
            w = _load_tile(u_ref, j_sm, t, c, scale=hh)
            pick = jnp.where(own_cols, sel[:, c * tile_cols:(c + 1) * tile_cols], 0.0).astype(BF16)
            r0 = k * PEER_HK + c * SLABS_PER_TILE
            part_ref[r0:r0 + SLABS_PER_TILE, :] = jnp.dot(pick, w, preferred_element_type=F32)

    def tokens(i, carry):
        finish(i * TOKEN_UNROLL)
        for k in range(TOKEN_UNROLL):
            token(i * TOKEN_UNROLL + k, k)
        return carry

    lax.fori_loop(0, tb // TOKEN_UNROLL, tokens, 0)
    finish(tb)
    a_ref[...] = _gelu_exact(zs_ref[TOKEN_UNROLL:, :]) * g_ref[...]


def _peer_v_kernel(j_sm, a_in_ref, p_ref, v_ref, o_ref, arep_ref, tgt_ref, *, tb):
    expand = _expand_matrix()
    par_rep = jnp.dot(p_ref[...].astype(BF16), expand, preferred_element_type=F32)
    tgt_ref[...] = _diag_target(par_rep)
    arep_ref[...] = jnp.dot(a_in_ref[...].astype(BF16), expand, preferred_element_type=F32)
    sub = lax.broadcasted_iota(I32, (PEER_ROWS, EXP_COLS), 0).astype(F32)

    def token(t):
        tgt = tgt_ref[pl.ds(t, 1), :]
        am = jnp.where(sub == tgt, arep_ref[pl.ds(t, 1), :], 0.0).astype(BF16)
        acc = jnp.zeros((PEER_ROWS, LANES), F32)
        for c in range(N_TILES):
            w = _load_tile(v_ref, j_sm, t, c)
            acc = acc + jnp.dot(am[:, c * 256:(c + 1) * 256], w, preferred_element_type=F32)
        o_ref[t] = acc

    def tokens(i, carry):
        for k in range(TOKEN_UNROLL):
            token(i * TOKEN_UNROLL + k)
        return carry

    lax.fori_loop(0, tb // TOKEN_UNROLL, tokens, 0)


def _table_spec():
    return pl.BlockSpec((PEER_EXPERTS // 2 * PEER_ROWS, LANES), lambda i: (0, 0),
                        pipeline_mode=pl.Buffered(1))


def _peer_u_call(jidx, h2, par, gate, u_slabs, tb):
    T = jidx.shape[0]
    row = pl.BlockSpec((tb, PEER_HK), lambda i: (i, 0))
    return pl.pallas_call(
        functools.partial(_peer_u_kernel, tb=tb),
        grid=(T // tb,),
        in_specs=[pl.BlockSpec((tb, PEER_HK), lambda i: (i, 0), memory_space=pltpu.SMEM),
                  pl.BlockSpec((tb, PEER_ROWS, LANES), lambda i: (i, 0, 0)),
                  row, row, _table_spec()],
        out_specs=row,
        out_shape=jax.ShapeDtypeStruct((T, PEER_HK), F32),
        scratch_shapes=[pltpu.VMEM((tb, EXP_COLS), F32), pltpu.VMEM((TOKEN_UNROLL * PEER_HK, LANES), F32),
                        pltpu.VMEM((tb + TOKEN_UNROLL, PEER_HK), F32)],
        compiler_params=_params(("arbitrary",)),
        name="peer_u",
    )(jidx, h2, par, gate, u_slabs)


def _peer_v_call(jidx, a, par, v_slabs, tb):
    T = jidx.shape[0]
    row = pl.BlockSpec((tb, PEER_HK), lambda i: (i, 0))
    return pl.pallas_call(
        functools.partial(_peer_v_kernel, tb=tb),
        grid=(T // tb,),
        in_specs=[pl.BlockSpec((tb, PEER_HK), lambda i: (i, 0), memory_space=pltpu.SMEM),
                  row, row, _table_spec()],
        out_specs=pl.BlockSpec((tb, PEER_ROWS, LANES), lambda i: (i, 0, 0)),
        out_shape=jax.ShapeDtypeStruct((T, PEER_ROWS, LANES), F32),
        scratch_shapes=[pltpu.VMEM((tb, EXP_COLS), F32), pltpu.VMEM((tb, EXP_COLS), F32)],
        compiler_params=_params(("arbitrary",)),
        name="peer_v",
    )(jidx, a, par, v_slabs)


def _ln2_kernel(x_ref, f_ref, g_ref, w_ref, b_ref, o_ref):
    o_ref[0] = _layer_norm(DEEP_ALPHA * x_ref[0] + (1.0 + g_ref[0]) * f_ref[0], w_ref[...], b_ref[...])


def _ln2_call(x1, ffn, g2, ln_w, ln_b, tm):
    B, S, D = x1.shape
    row = pl.BlockSpec((1, tm, D), lambda b, i: (b, i, 0))
    vec = pl.BlockSpec((1, D), lambda b, i: (0, 0))
    return pl.pallas_call(
        _ln2_kernel,
        grid=(B, S // tm),
        in_specs=[row, row, pl.BlockSpec((1, 1, D), lambda b, i: (b, 0, 0)), vec, vec],
        out_specs=row,
        out_shape=jax.ShapeDtypeStruct((B, S, D), F32),
        compiler_params=_params(("arbitrary", "arbitrary")),
        name="ln2",
    )(x1, ffn, g2, ln_w.reshape(1, -1), ln_b.reshape(1, -1))


def _tiles(B, S):
    return dict(inproj=min(S, 512), hgrn=min(S, 256), s5=min(S, 32), out=min(S, 256),
                topk=LANES, peer=LANES, ln2=min(S, 512))


def kernel(x, c, ada_w, ada_b, w_in, hg_lower_bound, hg_norm_w, ssm_a_re, ssm_a_im, ssm_log_dt,
           ssm_b_re, ssm_b_im, ssm_c_re, ssm_c_im, ssm_d, ssm_glu_w, ssm_glu_b, ssm_norm_w,
           w_out, ln1_w, ln1_b, peer_w_q, peer_sub_keys, peer_u, peer_v, ln2_w, ln2_b):
    B, S, D = x.shape
    T = B * S
    tl = _tiles(B, S)
    lb_all = jnp.cumsum(jax.nn.softmax(hg_lower_bound.astype(F32), axis=0), axis=0)
    for l in range(DEPTH):
        mod = _mod_call(c, ada_w[l], ada_b[l]).reshape(B, 6, 1, D)
        sh1, sc1, g1, sh2, sc2, g2 = (mod[:, n] for n in range(6))

        proj = _inproj_call(x, sc1, sh1, w_in[l].astype(BF16), tl["inproj"])
        o_hg = _hgrn_call(proj, lb_all[l], hg_norm_w[l], tl["hgrn"])
        lam_re, lam_im, bm, cm = _s5_prepare(ssm_a_re[l], ssm_a_im[l], ssm_log_dt[l], ssm_b_re[l],
                                             ssm_b_im[l], ssm_c_re[l], ssm_c_im[l])
        o_ssm = _s5_call(proj, lam_re, lam_im, bm, cm, ssm_d[l], ssm_glu_w[l].astype(BF16),
                         ssm_glu_b[l], ssm_norm_w[l], tl["s5"])
        x1, h2, q2 = _out_call(x, o_hg, o_ssm, g1, sc2, sh2, w_out[l].astype(BF16), ln1_w[l], ln1_b[l],
                               peer_w_q[l].astype(BF16), tl["out"])

        keys = peer_sub_keys[l].astype(BF16).reshape(2 * PEER_HEADS, PEER_NKEYS, PEER_HALF)
        jidx, par, gate = _topk_call(q2.reshape(T, PEER_HEADS * PEER_DKEY), keys, tl["topk"])
        u_slabs = _pack_slabs(peer_u[l])
        v_slabs = _pack_slabs(peer_v[l])
        a = _peer_u_call(jidx, h2.reshape(T, PEER_ROWS, LANES), par, gate, u_slabs, tl["peer"])
        ffn = _peer_v_call(jidx, a, par, v_slabs, tl["peer"])
        x = _ln2_call(x1, ffn.reshape(B, S, D), g2, ln2_w[l], ln2_b[l], tl["ln2"])
    return x
```

```python
import functools
import math

import jax
import jax.numpy as jnp
from jax import lax
from jax.experimental import pallas as pl
from jax.experimental.pallas import tpu as pltpu

F32 = jnp.float32
BF16 = jnp.bfloat16
I32 = jnp.int32

D_MODEL = 1024
DEPTH = 1
D_HGRN = 512
HG_HEADS = 4
HG_DK = 128
D_SSM = 512
SSM_GROUP = 16
SSM_GROUPS = 32
SSM_STATE = 64
IN_COLS = 4 * D_HGRN + D_SSM
PEER_HEADS = 8
PEER_NKEYS = 128
PEER_EXPERTS = PEER_NKEYS * PEER_NKEYS
PEER_DKEY = 256
PEER_HALF = 128
PEER_TOPK = 16
PEER_HK = PEER_HEADS * PEER_TOPK
DEEP_ALPHA = (2.0 * DEPTH) ** 0.25
LN_EPS = 1e-5
RMS_EPS = 1e-6

LANES = 128
SUBLANES = 8
HG_CHUNK = 16
NEG_BIG = -1e30
VMEM_LIMIT = 48 * 1024 * 1024


def _sigmoid(x):
    return 1.0 / (1.0 + jnp.exp(-x))


def _silu(x):
    return x * _sigmoid(x)


def _gelu_exact(x):
    return 0.5 * x * (1.0 + lax.erf(x * (1.0 / math.sqrt(2.0))))


def _params(sem):
    return pltpu.CompilerParams(dimension_semantics=sem, vmem_limit_bytes=VMEM_LIMIT)


def _mod_kernel(c_ref, w_ref, b_ref, o_ref):
    cond = _silu(c_ref[...])
    o_ref[...] = jnp.dot(cond, w_ref[...], preferred_element_type=F32,
                         precision=lax.Precision.HIGHEST) + b_ref[...]


def _mod_call(c, ada_w, ada_b):
    B, D = c.shape
    n = ada_w.shape[1] // D
    return pl.pallas_call(
        _mod_kernel,
        grid=(n,),
        in_specs=[pl.BlockSpec((B, D), lambda j: (0, 0)),
                  pl.BlockSpec((D, D), lambda j: (0, j)),
                  pl.BlockSpec((1, D), lambda j: (0, j))],
        out_specs=pl.BlockSpec((B, D), lambda j: (0, j)),
        out_shape=jax.ShapeDtypeStruct((B, n * D), F32),
        compiler_params=_params(("arbitrary",)),
        name="mod",
    )(c, ada_w, ada_b.reshape(1, -1))


def _inproj_kernel(x_ref, sc_ref, sh_ref, w_ref, o_ref):
    h = x_ref[0] * (1.0 + sc_ref[0]) + sh_ref[0]
    o_ref[0] = jnp.dot(h.astype(BF16), w_ref[...], preferred_element_type=F32)


def _inproj_call(x, sc1, sh1, w_in_bf16, tm):
    B, S, D = x.shape
    N = w_in_bf16.shape[1]
    return pl.pallas_call(
        _inproj_kernel,
        grid=(B, S // tm),
        in_specs=[pl.BlockSpec((1, tm, D), lambda b, i: (b, i, 0)),
                  pl.BlockSpec((1, 1, D), lambda b, i: (b, 0, 0)),
                  pl.BlockSpec((1, 1, D), lambda b, i: (b, 0, 0)),
                  pl.BlockSpec((D, N), lambda b, i: (0, 0))],
        out_specs=pl.BlockSpec((1, tm, N), lambda b, i: (b, i, 0)),
        out_shape=jax.ShapeDtypeStruct((B, S, N), F32),
        compiler_params=_params(("arbitrary", "arbitrary")),
        name="inproj",
    )(x, sc1, sh1, w_in_bf16)


def _hgrn_kernel(q_ref, f_ref, i_ref, g_ref, lb_ref, nw_ref, o_ref, st_ref, b_ref, *, ts):
    @pl.when(pl.program_id(1) == 0)
    def _():
        st_ref[...] = jnp.zeros_like(st_ref)

    lb_all = lb_ref[...]
    lf = jnp.log(lb_all + (1.0 - lb_all) * _sigmoid(f_ref[0]))
    r = lax.broadcasted_iota(I32, (ts, ts), 0)
    c = lax.broadcasted_iota(I32, (ts, ts), 1)
    tri = jnp.where((r // HG_CHUNK == c // HG_CHUNK) & (c <= r), 1.0, 0.0).astype(BF16)
    hi = lf.astype(BF16)
    r1 = lf - hi.astype(F32)
    mid = r1.astype(BF16)
    lo = (r1 - mid.astype(F32)).astype(BF16)
    b_ref[...] = (jnp.dot(tri, hi, preferred_element_type=F32)
                  + jnp.dot(tri, mid, preferred_element_type=F32)
                  + jnp.dot(tri, lo, preferred_element_type=F32))

    rows = lax.broadcasted_iota(I32, (HG_CHUNK, 1), 0)

    def head_chunk(sl, h):
        hs = slice(h * HG_DK, (h + 1) * HG_DK)
        lb = lb_ref[:, hs]
        q = q_ref[0, sl, hs]
        v = i_ref[0, sl, hs]
        g = g_ref[0, sl, hs]
        qs = _silu(q)
        k = 1.0 - (lb + (1.0 - lb) * _sigmoid(f_ref[0, sl, hs]))
        b = b_ref[sl, hs]
        bl = b[HG_CHUNK - 1:HG_CHUNK, :]
        qh = qs * jnp.exp(b)
        kh = k * jnp.exp(bl - b)
        st = st_ref[h]
        o = lax.dot_general(qh.astype(BF16), st.astype(BF16), (((1,), (1,)), ((), ())),
                            preferred_element_type=F32)
        for s in range(HG_CHUNK):
            e = jnp.exp(jnp.where(rows >= s, b - b[s:s + 1, :], NEG_BIG))
            col = jnp.sum(e * qs * k[s:s + 1, :], axis=1, keepdims=True)
            o = o + col * v[s:s + 1, :]
        ms = jnp.mean(o * o, axis=1, keepdims=True)
        o_ref[0, sl, hs] = o * lax.rsqrt(ms + RMS_EPS) * nw_ref[:, hs] * _silu(g)
        upd = lax.dot_general(v.astype(BF16), kh.astype(BF16), (((0,), (0,)), ((), ())),
                              preferred_element_type=F32)
        st_ref[h] = st * jnp.exp(bl) + upd

    def chunk(ci, carry):
        sl = pl.ds(pl.multiple_of(ci * HG_CHUNK, HG_CHUNK), HG_CHUNK)
        for h in range(HG_HEADS):
            head_chunk(sl, h)
        return carry

    lax.fori_loop(0, ts // HG_CHUNK, chunk, 0)


def _hgrn_call(proj, lb, norm_w, ts):
    B, S, _ = proj.shape

    def col(j):
        return pl.BlockSpec((1, ts, D_HGRN), lambda b, i, j=j: (b, i, j))

    vec = pl.BlockSpec((1, D_HGRN), lambda b, i: (0, 0))
    return pl.pallas_call(
        functools.partial(_hgrn_kernel, ts=ts),
        grid=(B, S // ts),
        in_specs=[col(0), col(1), col(2), col(3), vec, vec],
        out_specs=pl.BlockSpec((1, ts, D_HGRN), lambda b, i: (b, i, 0)),
        out_shape=jax.ShapeDtypeStruct((B, S, D_HGRN), F32),
        scratch_shapes=[pltpu.VMEM((HG_HEADS, HG_DK, HG_DK), F32), pltpu.VMEM((ts, D_HGRN), F32)],
        compiler_params=_params(("arbitrary", "arbitrary")),
        name="hgrn",
    )(proj, proj, proj, proj, lb.reshape(1, -1), norm_w.reshape(1, -1))


S5_CB = D_SSM // LANES
S5_NS = SSM_GROUPS * SSM_STATE
S5_SB = S5_NS // S5_CB


def _s5_kernel(u_ref, lre_ref, lim_ref, bm_ref, cm_ref, d_ref, gw_ref, gb_ref, nw_ref, o_ref,
               ut_ref, bu_ref, xs_ref, y_ref, *, nb, tt):
    @pl.when(pl.program_id(0) == 0)
    def _():
        xs_ref[...] = jnp.zeros_like(xs_ref)

    for b in range(nb):
        for cb in range(S5_CB):
            ut_ref[cb, pl.ds(b, tt, stride=nb), :] = u_ref[b, :, cb * LANES:(cb + 1) * LANES]
    for cb in range(S5_CB):
        res = jnp.dot(ut_ref[cb].astype(BF16), bm_ref[cb], preferred_element_type=F32)
        bu_ref[:, cb * S5_SB:(cb + 1) * S5_SB] = res[:, :S5_SB]
        bu_ref[:, S5_NS + cb * S5_SB:S5_NS + (cb + 1) * S5_SB] = res[:, S5_SB:]

    for sb in range(S5_CB):
        re = slice(sb * S5_SB, (sb + 1) * S5_SB)
        im = slice(S5_NS + sb * S5_SB, S5_NS + (sb + 1) * S5_SB)
        lr = lre_ref[:, re]
        li = lim_ref[:, re]

        def step(t, carry, re=re, im=im, lr=lr, li=li):
            xr, xi = carry
            rows = pl.ds(pl.multiple_of(t * nb, nb), nb)
            nxr = lr * xr - li * xi + bu_ref[rows, re]
            nxi = lr * xi + li * xr + bu_ref[rows, im]
            bu_ref[rows, re] = nxr
            bu_ref[rows, im] = nxi
            return nxr, nxi

        xr, xi = lax.fori_loop(0, tt, step, (xs_ref[:, re], xs_ref[:, im]), unroll=4)
        xs_ref[:, re] = xr
        xs_ref[:, im] = xi

    ys = []
    for cb in range(S5_CB):
        xre = bu_ref[:, cb * S5_SB:(cb + 1) * S5_SB].astype(BF16)
        xim = bu_ref[:, S5_NS + cb * S5_SB:S5_NS + (cb + 1) * S5_SB].astype(BF16)
        ys.append(jnp.dot(xre, cm_ref[cb, :S5_SB, :], preferred_element_type=F32)
                  + jnp.dot(xim, cm_ref[cb, S5_SB:, :], preferred_element_type=F32)
                  + d_ref[:, cb * LANES:(cb + 1) * LANES] * ut_ref[cb])
    y = _gelu_exact(jnp.concatenate(ys, axis=1))
    z = jnp.dot(y.astype(BF16), gw_ref[...], preferred_element_type=F32) + gb_ref[...]
    y = y * _sigmoid(z)
    y = y * lax.rsqrt(jnp.mean(y * y, axis=1, keepdims=True) + RMS_EPS) * nw_ref[...]
    for cb in range(S5_CB):
        y_ref[cb] = y[:, cb * LANES:(cb + 1) * LANES]
    for b in range(nb):
        for cb in range(S5_CB):
            o_ref[b, :, cb * LANES:(cb + 1) * LANES] = y_ref[cb, pl.ds(b, tt, stride=nb), :]


def _s5_prepare(a_re, a_im, log_dt, b_re, b_im, c_re, c_im):
    ar, ai = a_re.astype(F32), a_im.astype(F32)
    dt = jnp.exp(log_dt.astype(F32))[:, None]
    mag = jnp.exp(ar * dt)
    lam_re = mag * jnp.cos(ai * dt)
    lam_im = mag * jnp.sin(ai * dt)
    den = ar * ar + ai * ai
    nr, ni = lam_re - 1.0, lam_im
    z_re = (nr * ar + ni * ai) / den
    z_im = (ni * ar - nr * ai) / den
    br, bi = b_re.astype(F32), b_im.astype(F32)
    bb_re = z_re[..., None] * br - z_im[..., None] * bi
    bb_im = z_re[..., None] * bi + z_im[..., None] * br
    gl = SSM_GROUPS // S5_CB
    eye = jnp.eye(gl, dtype=F32)

    def in_block(m):
        m = m.reshape(S5_CB, gl, SSM_STATE, SSM_GROUP)
        return jnp.einsum('cgpi,gh->cgihp', m, eye).reshape(S5_CB, gl * SSM_GROUP, gl * SSM_STATE)

    def out_block(m):
        m = m.reshape(S5_CB, gl, SSM_GROUP, SSM_STATE)
        return jnp.einsum('cgip,gh->cgphi', m, eye).reshape(S5_CB, gl * SSM_STATE, gl * SSM_GROUP)

    bm = jnp.concatenate([in_block(bb_re), in_block(bb_im)], axis=2).astype(BF16)
    cm = jnp.concatenate([out_block(c_re.astype(F32)), -out_block(c_im.astype(F32))], axis=1).astype(BF16)
    return lam_re.reshape(1, -1), lam_im.reshape(1, -1), bm, cm


def _s5_call(proj, lam_re, lam_im, bm, cm, d_skip, glu_w_bf16, glu_b, norm_w, tt):
    B, S, _ = proj.shape
    R = tt * B
    ucol = (4 * D_HGRN) // D_SSM
    full = lambda shape: pl.BlockSpec(shape, lambda i: (0,) * len(shape))
    return pl.pallas_call(
        functools.partial(_s5_kernel, nb=B, tt=tt),
        grid=(S // tt,),
        in_specs=[pl.BlockSpec((B, tt, D_SSM), lambda i: (0, i, ucol)),
                  full((1, S5_NS)), full((1, S5_NS)),
                  full((S5_CB, LANES, 2 * S5_SB)), full((S5_CB, 2 * S5_SB, LANES)),
                  full((1, D_SSM)), full((D_SSM, D_SSM)), full((1, D_SSM)), full((1, D_SSM))],
        out_specs=pl.BlockSpec((B, tt, D_SSM), lambda i: (0, i, 0)),
        out_shape=jax.ShapeDtypeStruct((B, S, D_SSM), F32),
        scratch_shapes=[pltpu.VMEM((S5_CB, R, LANES), F32), pltpu.VMEM((R, 2 * S5_NS), F32),
                        pltpu.VMEM((B, 2 * S5_NS), F32), pltpu.VMEM((S5_CB, R, LANES), F32)],
        compiler_params=_params(("arbitrary",)),
        name="s5",
    )(proj, lam_re, lam_im, bm, cm, d_skip.reshape(1, -1), glu_w_bf16, glu_b.reshape(1, -1),
      norm_w.reshape(1, -1))


def _layer_norm(y, w, b):
    mu = jnp.mean(y, axis=-1, keepdims=True)
    yc = y - mu
    var = jnp.mean(yc * yc, axis=-1, keepdims=True)
    return yc * lax.rsqrt(var + LN_EPS) * w + b


def _out_kernel(x_ref, hg_ref, ss_ref, g1_ref, sc2_ref, sh2_ref, wo_ref, lw_ref, lb_ref, wq_ref,
                x1_ref, h2_ref, q2_ref):
    mixed = (jnp.dot(hg_ref[0].astype(BF16), wo_ref[:D_HGRN, :], preferred_element_type=F32)
             + jnp.dot(ss_ref[0].astype(BF16), wo_ref[D_HGRN:, :], preferred_element_type=F32))
    x1 = _layer_norm(DEEP_ALPHA * x_ref[0] + (1.0 + g1_ref[0]) * mixed, lw_ref[...], lb_ref[...])
    x1_ref[0] = x1
    h2 = x1 * (1.0 + sc2_ref[0]) + sh2_ref[0]
    h2_ref[0] = h2
    q2_ref[0] = jnp.dot(h2.astype(BF16), wq_ref[...], preferred_element_type=F32)


def _out_call(x, o_hg, o_ssm, g1, sc2, sh2, w_out_bf16, ln_w, ln_b, w_q_bf16, tm):
    B, S, D = x.shape
    NQ = w_q_bf16.shape[1]
    row = lambda n: pl.BlockSpec((1, tm, n), lambda b, i: (b, i, 0))
    mod = pl.BlockSpec((1, 1, D), lambda b, i: (b, 0, 0))
    full = lambda shape: pl.BlockSpec(shape, lambda b, i: (0,) * len(shape))
    return pl.pallas_call(
        _out_kernel,
        grid=(B, S // tm),
        in_specs=[row(D), row(D_HGRN), row(D_SSM), mod, mod, mod,
                  full((D, D)), full((1, D)), full((1, D)), full((D, NQ))],
        out_specs=[row(D), row(D), row(NQ)],
        out_shape=[jax.ShapeDtypeStruct((B, S, D), F32), jax.ShapeDtypeStruct((B, S, D), F32),
                   jax.ShapeDtypeStruct((B, S, NQ), F32)],
        compiler_params=_params(("arbitrary", "arbitrary")),
        name="outproj",
    )(x, o_hg, o_ssm, g1, sc2, sh2, w_out_bf16, ln_w.reshape(1, -1), ln_b.reshape(1, -1), w_q_bf16)


_CAND_PIECES = ((0, 1, 0, 16, 0), (1, 1, 0, 8, 0), (2, 1, 0, 8, 0), (3, 1, 0, 8, 0),
                (8, 8, 0, 1, 0), (0, 8, 0, 1, 4), (0, 8, 1, 1, 4), (0, 8, 2, 1, 4))


def _topk_kernel(q_ref, k_ref, j_ref, p_ref, g_ref, *, tb):
    nk = PEER_NKEYS
    kiota = lax.broadcasted_iota(I32, (nk, tb), 0)
    ninf = jnp.float32(-jnp.inf)

    def top16(sc):
        vals, idxs = [], []
        for _ in range(PEER_TOPK):
            m = jnp.max(sc, axis=0, keepdims=True)
            idx = jnp.min(jnp.where(sc == m, kiota, nk), axis=0, keepdims=True)
            sc = jnp.where(kiota == idx, ninf, sc)
            vals.append(m)
            idxs.append(idx)
        return jnp.concatenate(vals, axis=0), jnp.concatenate(idxs, axis=0)

    flat_rows = []
    for a0, na, b0, nb, _ in _CAND_PIECES:
        for a in range(a0, a0 + na):
            for b in range(b0, b0 + nb):
                flat_rows.append(a * PEER_TOPK + b)
    ncand = len(flat_rows)
    riota = lax.broadcasted_iota(I32, (ncand, 1), 0)
    cflat = jnp.zeros((ncand, 1), I32)
    for rix, fv in enumerate(flat_rows):
        cflat = jnp.where(riota == rix, fv, cflat)

    e_rows, g_rows = [], []
    for h in range(PEER_HEADS):
        tops, topi = [], []
        for p in range(2):
            hp = 2 * h + p
            qhp = q_ref[:, hp * PEER_HALF:(hp + 1) * PEER_HALF].astype(BF16)
            sc = lax.dot_general(k_ref[hp], qhp, (((1,), (1,)), ((), ())),
                                 preferred_element_type=F32)
            v_, i_ = top16(sc)
            tops.append(v_)
            topi.append(i_)
        s1, s2 = tops
        i1, i2 = topi
        cv, ce = [], []
        for a0, na, b0, nb, mask_below in _CAND_PIECES:
            val = s1[a0:a0 + na, :] + s2[b0:b0 + nb, :]
            eid = i1[a0:a0 + na, :] * PEER_NKEYS + i2[b0:b0 + nb, :]
            if mask_below:
                prow = lax.broadcasted_iota(I32, (max(na, nb), 1), 0)
                val = jnp.where(prow < mask_below, ninf, val)
            cv.append(val)
            ce.append(eid)
        cand = jnp.concatenate(cv, axis=0)
        cexp = jnp.concatenate(ce, axis=0)
        bv, be = [], []
        for _ in range(PEER_TOPK):
            m = jnp.max(cand, axis=0, keepdims=True)
            cm = jnp.min(jnp.where(cand == m, cflat, PEER_TOPK * PEER_TOPK), axis=0, keepdims=True)
            sel = cflat == cm
            be.append(jnp.max(jnp.where(sel, cexp, -1), axis=0, keepdims=True))
            cand = jnp.where(sel, ninf, cand)
            bv.append(m)
        best = jnp.concatenate(bv, axis=0)
        ex = jnp.exp(best - best[0:1, :])
        g_rows.append(ex / jnp.sum(ex, axis=0, keepdims=True))
        e_rows.append(jnp.concatenate(be, axis=0))
    expert = jnp.concatenate(e_rows, axis=0)
    gate = jnp.concatenate(g_rows, axis=0)
    j_ref[...] = (expert >> 1) * PEER_ROWS
    p_ref[...] = jnp.transpose((expert & 1).astype(F32))
    g_ref[...] = jnp.transpose(gate)


def _topk_call(q2, keys_bf16, tb):
    T = q2.shape[0]
    out = lambda dt: jax.ShapeDtypeStruct((T, PEER_HK), dt)
    ospec = pl.BlockSpec((tb, PEER_HK), lambda i: (i, 0))
    return pl.pallas_call(
        functools.partial(_topk_kernel, tb=tb),
        grid=(T // tb,),
        in_specs=[pl.BlockSpec((tb, PEER_HEADS * PEER_DKEY), lambda i: (i, 0)),
                  pl.BlockSpec((2 * PEER_HEADS, PEER_NKEYS, PEER_HALF), lambda i: (0, 0, 0))],
        out_specs=[pl.BlockSpec((PEER_HK, tb), lambda i: (0, i)), ospec, ospec],
        out_shape=[jax.ShapeDtypeStruct((PEER_HK, T), I32), out(F32), out(F32)],
        compiler_params=_params(("arbitrary",)),
        name="topk",
    )(q2, keys_bf16)


PEER_ROWS = D_MODEL // LANES
SLAB_ROWS = 2 * PEER_ROWS
SLABS_PER_TILE = 16
N_TILES = PEER_HK // SLABS_PER_TILE
EXP_COLS = PEER_HK * SLAB_ROWS
TOKEN_UNROLL = 8


def _pack_slabs(table):
    t = table.reshape(PEER_EXPERTS // 2, 2, D_MODEL)
    bits = lambda x: lax.bitcast_convert_type(x.astype(BF16).astype(F32), jnp.uint32)
    words = (bits(t[:, 0]) >> 16) | (bits(t[:, 1]) & jnp.uint32(0xFFFF0000))
    return words.reshape(PEER_EXPERTS // 2 * PEER_ROWS, LANES)


def _load_tile(tab_ref, j_sm, t, c, scale=None):
    slabs = []
    for m in range(SLABS_PER_TILE):
        off = pl.multiple_of(j_sm.at[c * SLABS_PER_TILE + m][t], PEER_ROWS)
        slab = pltpu.bitcast(tab_ref[pl.ds(off, PEER_ROWS), :], BF16)
        slabs.append(slab if scale is None else slab * scale)
    return jnp.concatenate(slabs, axis=0)


def _expand_matrix():
    r = lax.broadcasted_iota(I32, (PEER_HK, EXP_COLS), 0)
    c = lax.broadcasted_iota(I32, (PEER_HK, EXP_COLS), 1)
    return jnp.where(c // SLAB_ROWS == r, 1.0, 0.0).astype(BF16)


def _diag_target(par_rep):
    q = lax.broadcasted_iota(I32, par_rep.shape, 1) % SLAB_ROWS
    return jnp.where((q % 2).astype(F32) == par_rep, (q // 2).astype(F32), -1.0)


def _peer_u_kernel(j_sm, h_ref, p_ref, g_ref, u_ref, a_ref, sel_ref, part_ref, zs_ref, *, tb):
    expand = _expand_matrix()
    par_rep = jnp.dot(p_ref[...].astype(BF16), expand, preferred_element_type=F32)
    q = lax.broadcasted_iota(I32, (tb, EXP_COLS), 1) % SLAB_ROWS
    sel_ref[...] = jnp.where((q % 2).astype(F32) == par_rep, 1.0, 0.0)
    tile_cols = SLABS_PER_TILE * SLAB_ROWS
    slot_of_col = lax.broadcasted_iota(I32, (SLABS_PER_TILE, tile_cols), 1) // SLAB_ROWS
    own_cols = slot_of_col == lax.broadcasted_iota(I32, (SLABS_PER_TILE, tile_cols), 0)
    part_ref[...] = jnp.zeros_like(part_ref)

    def finish(row0):
        for k in range(TOKEN_UNROLL):
            part = part_ref[k * PEER_HK:(k + 1) * PEER_HK, :]
            zs_ref[pl.ds(row0 + k, 1), :] = jnp.sum(jnp.transpose(part), axis=0, keepdims=True)

    def token(t, k):
        hbits = lax.bitcast_convert_type(h_ref[t].astype(BF16).astype(F32), jnp.uint32)
        hh = pltpu.bitcast((hbits >> 16) | hbits, BF16)
        sel = sel_ref[pl.ds(t, 1), :]
        for c in range(N_TILES):
            w = _load_tile(u_ref, j_sm, t, c, scale=hh)
            pick = jnp.where(own_cols, sel[:, c * tile_cols:(c + 1) * tile_cols], 0.0).astype(BF16)
            r0 = k * PEER_HK + c * SLABS_PER_TILE
            part_ref[r0:r0 + SLABS_PER_TILE, :] = jnp.dot(pick, w, preferred_element_type=F32)

    def tokens(i, carry):
        finish(i * TOKEN_UNROLL)
        for k in range(TOKEN_UNROLL):
            token(i * TOKEN_UNROLL + k, k)
        return carry

    lax.fori_loop(0, tb // TOKEN_UNROLL, tokens, 0)
    finish(tb)
    a_ref[...] = _gelu_exact(zs_ref[TOKEN_UNROLL:, :]) * g_ref[...]


def _peer_v_kernel(j_sm, a_in_ref, p_ref, v_ref, o_ref, arep_ref, tgt_ref, *, tb):
    expand = _expand_matrix()
    par_rep = jnp.dot(p_ref[...].astype(BF16), expand, preferred_element_type=F32)
    tgt_ref[...] = _diag_target(par_rep)
    arep_ref[...] = jnp.dot(a_in_ref[...].astype(BF16), expand, preferred_element_type=F32)
    sub = lax.broadcasted_iota(I32, (PEER_ROWS, EXP_COLS), 0).astype(F32)

    def token(t):
        tgt = tgt_ref[pl.ds(t, 1), :]
        am = jnp.where(sub == tgt, arep_ref[pl.ds(t, 1), :], 0.0).astype(BF16)
        acc = jnp.zeros((PEER_ROWS, LANES), F32)
        for c in range(N_TILES):
            w = _load_tile(v_ref, j_sm, t, c)
            acc = acc + jnp.dot(am[:, c * 256:(c + 1) * 256], w, preferred_element_type=F32)
        o_ref[t] = acc

    def tokens(i, carry):
        for k in range(TOKEN_UNROLL):
            token(i * TOKEN_UNROLL + k)
        return carry

    lax.fori_loop(0, tb // TOKEN_UNROLL, tokens, 0)


def _slot_index_spec(tb):
    return pl.BlockSpec((PEER_HK, tb), lambda i: (0, i), memory_space=pltpu.SMEM,
                        pipeline_mode=pl.Buffered(1))


def _table_spec():
    return pl.BlockSpec((PEER_EXPERTS // 2 * PEER_ROWS, LANES), lambda i: (0, 0),
                        pipeline_mode=pl.Buffered(1))


def _peer_u_call(jidx, h2, par, gate, u_slabs, tb):
    T = jidx.shape[1]
    row = pl.BlockSpec((tb, PEER_HK), lambda i: (i, 0))
    return pl.pallas_call(
        functools.partial(_peer_u_kernel, tb=tb),
        grid=(T // tb,),
        in_specs=[_slot_index_spec(tb),
                  pl.BlockSpec((tb, PEER_ROWS, LANES), lambda i: (i, 0, 0)),
                  row, row, _table_spec()],
        out_specs=row,
        out_shape=jax.ShapeDtypeStruct((T, PEER_HK), F32),
        scratch_shapes=[pltpu.VMEM((tb, EXP_COLS), F32), pltpu.VMEM((TOKEN_UNROLL * PEER_HK, LANES), F32),
                        pltpu.VMEM((tb + TOKEN_UNROLL, PEER_HK), F32)],
        compiler_params=_params(("arbitrary",)),
        name="peer_u",
    )(jidx, h2, par, gate, u_slabs)


def _peer_v_call(jidx, a, par, v_slabs, tb):
    T = jidx.shape[1]
    row = pl.BlockSpec((tb, PEER_HK), lambda i: (i, 0))
    return pl.pallas_call(
        functools.partial(_peer_v_kernel, tb=tb),
        grid=(T // tb,),
        in_specs=[_slot_index_spec(tb),
                  row, row, _table_spec()],
        out_specs=pl.BlockSpec((tb, PEER_ROWS, LANES), lambda i: (i, 0, 0)),
        out_shape=jax.ShapeDtypeStruct((T, PEER_ROWS, LANES), F32),
        scratch_shapes=[pltpu.VMEM((tb, EXP_COLS), F32), pltpu.VMEM((tb, EXP_COLS), F32)],
        compiler_params=_params(("arbitrary",)),
        name="peer_v",
    )(jidx, a, par, v_slabs)


def _ln2_kernel(x_ref, f_ref, g_ref, w_ref, b_ref, o_ref):
    o_ref[0] = _layer_norm(DEEP_ALPHA * x_ref[0] + (1.0 + g_ref[0]) * f_ref[0], w_ref[...], b_ref[...])


def _ln2_call(x1, ffn, g2, ln_w, ln_b, tm):
    B, S, D = x1.shape
    row = pl.BlockSpec((1, tm, D), lambda b, i: (b, i, 0))
    vec = pl.BlockSpec((1, D), lambda b, i: (0, 0))
    return pl.pallas_call(
        _ln2_kernel,
        grid=(B, S // tm),
        in_specs=[row, row, pl.BlockSpec((1, 1, D), lambda b, i: (b, 0, 0)), vec, vec],
        out_specs=row,
        out_shape=jax.ShapeDtypeStruct((B, S, D), F32),
        compiler_params=_params(("arbitrary", "arbitrary")),
        name="ln2",
    )(x1, ffn, g2, ln_w.reshape(1, -1), ln_b.reshape(1, -1))


def _tiles(B, S):
    return dict(inproj=min(S, 512), hgrn=min(S, 256), s5=min(S, 32), out=min(S, 256),
                topk=LANES, peer=LANES, ln2=min(S, 512))


def kernel(x, c, ada_w, ada_b, w_in, hg_lower_bound, hg_norm_w, ssm_a_re, ssm_a_im, ssm_log_dt,
           ssm_b_re, ssm_b_im, ssm_c_re, ssm_c_im, ssm_d, ssm_glu_w, ssm_glu_b, ssm_norm_w,
           w_out, ln1_w, ln1_b, peer_w_q, peer_sub_keys, peer_u, peer_v, ln2_w, ln2_b):
    B, S, D = x.shape
    T = B * S
    tl = _tiles(B, S)
    lb_all = jnp.cumsum(jax.nn.softmax(hg_lower_bound.astype(F32), axis=0), axis=0)
    for l in range(DEPTH):
        mod = _mod_call(c, ada_w[l], ada_b[l]).reshape(B, 6, 1, D)
        sh1, sc1, g1, sh2, sc2, g2 = (mod[:, n] for n in range(6))

        proj = _inproj_call(x, sc1, sh1, w_in[l].astype(BF16), tl["inproj"])
        o_hg = _hgrn_call(proj, lb_all[l], hg_norm_w[l], tl["hgrn"])
        lam_re, lam_im, bm, cm = _s5_prepare(ssm_a_re[l], ssm_a_im[l], ssm_log_dt[l], ssm_b_re[l],
                                             ssm_b_im[l], ssm_c_re[l], ssm_c_im[l])
        o_ssm = _s5_call(proj, lam_re, lam_im, bm, cm, ssm_d[l], ssm_glu_w[l].astype(BF16),
                         ssm_glu_b[l], ssm_norm_w[l], tl["s5"])
        x1, h2, q2 = _out_call(x, o_hg, o_ssm, g1, sc2, sh2, w_out[l].astype(BF16), ln1_w[l], ln1_b[l],
                               peer_w_q[l].astype(BF16), tl["out"])

        keys = peer_sub_keys[l].astype(BF16).reshape(2 * PEER_HEADS, PEER_NKEYS, PEER_HALF)
        jidx, par, gate = _topk_call(q2.reshape(T, PEER_HEADS * PEER_DKEY), keys, tl["topk"])
        u_slabs = _pack_slabs(peer_u[l])
        v_slabs = _pack_slabs(peer_v[l])
        a = _peer_u_call(jidx, h2.reshape(T, PEER_ROWS, LANES), par, gate, u_slabs, tl["peer"])
        ffn = _peer_v_call(jidx, a, par, v_slabs, tl["peer"])
        x = _ln2_call(x1, ffn.reshape(B, S, D), g2, ln2_w[l], ln2_b[l], tl["ln2"])
    return x
```

```python
import functools
import math

import jax
import jax.numpy as jnp
from jax import lax
from jax.experimental import pallas as pl
from jax.experimental.pallas import tpu as pltpu

F32 = jnp.float32
BF16 = jnp.bfloat16
I32 = jnp.int32

D_MODEL = 1024
DEPTH = 1
D_HGRN = 512
HG_HEADS = 4
HG_DK = 128
D_SSM = 512
SSM_GROUP = 16
SSM_GROUPS = 32
SSM_STATE = 64
IN_COLS = 4 * D_HGRN + D_SSM
PEER_HEADS = 8
PEER_NKEYS = 128
PEER_EXPERTS = PEER_NKEYS * PEER_NKEYS
PEER_DKEY = 256
PEER_HALF = 128
PEER_TOPK = 16
PEER_HK = PEER_HEADS * PEER_TOPK
DEEP_ALPHA = (2.0 * DEPTH) ** 0.25
LN_EPS = 1e-5
RMS_EPS = 1e-6

LANES = 128
SUBLANES = 8
HG_CHUNK = 16
NEG_BIG = -1e30
VMEM_LIMIT = 48 * 1024 * 1024


def _sigmoid(x):
    return 1.0 / (1.0 + jnp.exp(-x))


def _silu(x):
    return x * _sigmoid(x)


def _gelu_exact(x):
    return 0.5 * x * (1.0 + lax.erf(x * (1.0 / math.sqrt(2.0))))


def _params(sem):
    return pltpu.CompilerParams(dimension_semantics=sem, vmem_limit_bytes=VMEM_LIMIT)


def _mod_kernel(c_ref, w_ref, b_ref, o_ref):
    cond = _silu(c_ref[...])
    o_ref[...] = jnp.dot(cond, w_ref[...], preferred_element_type=F32,
                         precision=lax.Precision.HIGHEST) + b_ref[...]


def _mod_call(c, ada_w, ada_b):
    B, D = c.shape
    n = ada_w.shape[1] // D
    return pl.pallas_call(
        _mod_kernel,
        grid=(n,),
        in_specs=[pl.BlockSpec((B, D), lambda j: (0, 0)),
                  pl.BlockSpec((D, D), lambda j: (0, j)),
                  pl.BlockSpec((1, D), lambda j: (0, j))],
        out_specs=pl.BlockSpec((B, D), lambda j: (0, j)),
        out_shape=jax.ShapeDtypeStruct((B, n * D), F32),
        compiler_params=_params(("arbitrary",)),
        name="mod",
    )(c, ada_w, ada_b.reshape(1, -1))


def _inproj_kernel(x_ref, sc_ref, sh_ref, w_ref, o_ref):
    h = x_ref[0] * (1.0 + sc_ref[0]) + sh_ref[0]
    o_ref[0] = jnp.dot(h.astype(BF16), w_ref[...], preferred_element_type=F32)


def _inproj_call(x, sc1, sh1, w_in_bf16, tm):
    B, S, D = x.shape
    N = w_in_bf16.shape[1]
    return pl.pallas_call(
        _inproj_kernel,
        grid=(B, S // tm),
        in_specs=[pl.BlockSpec((1, tm, D), lambda b, i: (b, i, 0)),
                  pl.BlockSpec((1, 1, D), lambda b, i: (b, 0, 0)),
                  pl.BlockSpec((1, 1, D), lambda b, i: (b, 0, 0)),
                  pl.BlockSpec((D, N), lambda b, i: (0, 0))],
        out_specs=pl.BlockSpec((1, tm, N), lambda b, i: (b, i, 0)),
        out_shape=jax.ShapeDtypeStruct((B, S, N), F32),
        compiler_params=_params(("arbitrary", "arbitrary")),
        name="inproj",
    )(x, sc1, sh1, w_in_bf16)


def _hgrn_kernel(q_ref, f_ref, i_ref, g_ref, lb_ref, nw_ref, o_ref, st_ref, b_ref, *, ts, nbt):
    @pl.when(pl.program_id(1) == 0)
    def _():
        st_ref[...] = jnp.zeros_like(st_ref)

    lb_all = lb_ref[...]
    r = lax.broadcasted_iota(I32, (ts, ts), 0)
    c = lax.broadcasted_iota(I32, (ts, ts), 1)
    tri = jnp.where((r // HG_CHUNK == c // HG_CHUNK) & (c <= r), 1.0, 0.0).astype(BF16)
    for bi in range(nbt):
        lf = jnp.log(lb_all + (1.0 - lb_all) * _sigmoid(f_ref[bi]))
        hi = lf.astype(BF16)
        r1 = lf - hi.astype(F32)
        mid = r1.astype(BF16)
        lo = (r1 - mid.astype(F32)).astype(BF16)
        b_ref[bi] = (jnp.dot(tri, hi, preferred_element_type=F32)
                     + jnp.dot(tri, mid, preferred_element_type=F32)
                     + jnp.dot(tri, lo, preferred_element_type=F32))

    half = HG_CHUNK // 2
    rows = lax.broadcasted_iota(I32, (HG_CHUNK, 1), 0)

    def head_chunk(sl, bi, h):
        hs = slice(h * HG_DK, (h + 1) * HG_DK)
        lb = lb_ref[:, hs]
        q = q_ref[bi, sl, hs]
        v = i_ref[bi, sl, hs]
        g = g_ref[bi, sl, hs]
        qs = _silu(q)
        k = 1.0 - (lb + (1.0 - lb) * _sigmoid(f_ref[bi, sl, hs]))
        b = b_ref[bi, sl, hs]
        bl = b[HG_CHUNK - 1:HG_CHUNK, :]
        qh = qs * jnp.exp(b)
        kh = k * jnp.exp(bl - b)
        st = st_ref[bi, h]
        o = lax.dot_general(qh.astype(BF16), st.astype(BF16), (((1,), (1,)), ((), ())),
                            preferred_element_type=F32)
        o_lo, o_hi = o[:half], o[half:]
        for s in range(HG_CHUNK):
            lo_rows = 0 if s < half else half
            bs, qss, rs = b[lo_rows:], qs[lo_rows:], rows[lo_rows:]
            e = jnp.exp(jnp.where(rs >= s, bs - b[s:s + 1, :], NEG_BIG))
            contrib = jnp.sum(e * qss * k[s:s + 1, :], axis=1, keepdims=True) * v[s:s + 1, :]
            if s < half:
                o_lo = o_lo + contrib[:half]
                o_hi = o_hi + contrib[half:]
            else:
                o_hi = o_hi + contrib
        o = jnp.concatenate([o_lo, o_hi], axis=0)
        ms = jnp.mean(o * o, axis=1, keepdims=True)
        o_ref[bi, sl, hs] = o * lax.rsqrt(ms + RMS_EPS) * nw_ref[:, hs] * _silu(g)
        upd = lax.dot_general(v.astype(BF16), kh.astype(BF16), (((0,), (0,)), ((), ())),
                              preferred_element_type=F32)
        st_ref[bi, h] = st * jnp.exp(bl) + upd

    def chunk(ci, carry):
        sl = pl.ds(pl.multiple_of(ci * HG_CHUNK, HG_CHUNK), HG_CHUNK)
        for bi in range(nbt):
            for h in range(HG_HEADS):
                head_chunk(sl, bi, h)
        return carry

    lax.fori_loop(0, ts // HG_CHUNK, chunk, 0)


def _hgrn_call(proj, lb, norm_w, ts, nbt):
    B, S, _ = proj.shape

    def col(j):
        return pl.BlockSpec((nbt, ts, D_HGRN), lambda b, i, j=j: (b, i, j))

    vec = pl.BlockSpec((1, D_HGRN), lambda b, i: (0, 0))
    return pl.pallas_call(
        functools.partial(_hgrn_kernel, ts=ts, nbt=nbt),
        grid=(B // nbt, S // ts),
        in_specs=[col(0), col(1), col(2), col(3), vec, vec],
        out_specs=pl.BlockSpec((nbt, ts, D_HGRN), lambda b, i: (b, i, 0)),
        out_shape=jax.ShapeDtypeStruct((B, S, D_HGRN), F32),
        scratch_shapes=[pltpu.VMEM((nbt, HG_HEADS, HG_DK, HG_DK), F32), pltpu.VMEM((nbt, ts, D_HGRN), F32)],
        compiler_params=_params(("arbitrary", "arbitrary")),
        name="hgrn",
    )(proj, proj, proj, proj, lb.reshape(1, -1), norm_w.reshape(1, -1))


S5_CB = D_SSM // LANES
S5_NS = SSM_GROUPS * SSM_STATE
S5_SB = S5_NS // S5_CB


def _s5_kernel(u_ref, lre_ref, lim_ref, bm_ref, cm_ref, d_ref, gw_ref, gb_ref, nw_ref, o_ref,
               ut_ref, bu_ref, xs_ref, y_ref, *, nb, tt):
    @pl.when(pl.program_id(0) == 0)
    def _():
        xs_ref[...] = jnp.zeros_like(xs_ref)

    for b in range(nb):
        for cb in range(S5_CB):
            ut_ref[cb, pl.ds(b, tt, stride=nb), :] = u_ref[b, :, cb * LANES:(cb + 1) * LANES]
    for cb in range(S5_CB):
        res = jnp.dot(ut_ref[cb].astype(BF16), bm_ref[cb], preferred_element_type=F32)
        bu_ref[:, cb * S5_SB:(cb + 1) * S5_SB] = res[:, :S5_SB]
        bu_ref[:, S5_NS + cb * S5_SB:S5_NS + (cb + 1) * S5_SB] = res[:, S5_SB:]

    for sb in range(S5_CB):
        re = slice(sb * S5_SB, (sb + 1) * S5_SB)
        im = slice(S5_NS + sb * S5_SB, S5_NS + (sb + 1) * S5_SB)
        lr = lre_ref[:, re]
        li = lim_ref[:, re]

        def step(t, carry, re=re, im=im, lr=lr, li=li):
            xr, xi = carry
            rows = pl.ds(pl.multiple_of(t * nb, nb), nb)
            nxr = lr * xr - li * xi + bu_ref[rows, re]
            nxi = lr * xi + li * xr + bu_ref[rows, im]
            bu_ref[rows, re] = nxr
            bu_ref[rows, im] = nxi
            return nxr, nxi

        xr, xi = lax.fori_loop(0, tt, step, (xs_ref[:, re], xs_ref[:, im]), unroll=4)
        xs_ref[:, re] = xr
        xs_ref[:, im] = xi

    ys = []
    for cb in range(S5_CB):
        xre = bu_ref[:, cb * S5_SB:(cb + 1) * S5_SB].astype(BF16)
        xim = bu_ref[:, S5_NS + cb * S5_SB:S5_NS + (cb + 1) * S5_SB].astype(BF16)
        ys.append(jnp.dot(xre, cm_ref[cb, :S5_SB, :], preferred_element_type=F32)
                  + jnp.dot(xim, cm_ref[cb, S5_SB:, :], preferred_element_type=F32)
                  + d_ref[:, cb * LANES:(cb + 1) * LANES] * ut_ref[cb])
    y = _gelu_exact(jnp.concatenate(ys, axis=1))
    z = jnp.dot(y.astype(BF16), gw_ref[...], preferred_element_type=F32) + gb_ref[...]
    y = y * _sigmoid(z)
    y = y * lax.rsqrt(jnp.mean(y * y, axis=1, keepdims=True) + RMS_EPS) * nw_ref[...]
    for cb in range(S5_CB):
        y_ref[cb] = y[:, cb * LANES:(cb + 1) * LANES]
    for b in range(nb):
        for cb in range(S5_CB):
            o_ref[b, :, cb * LANES:(cb + 1) * LANES] = y_ref[cb, pl.ds(b, tt, stride=nb), :]


def _s5_prepare(a_re, a_im, log_dt, b_re, b_im, c_re, c_im):
    ar, ai = a_re.astype(F32), a_im.astype(F32)
    dt = jnp.exp(log_dt.astype(F32))[:, None]
    mag = jnp.exp(ar * dt)
    lam_re = mag * jnp.cos(ai * dt)
    lam_im = mag * jnp.sin(ai * dt)
    den = ar * ar + ai * ai
    nr, ni = lam_re - 1.0, lam_im
    z_re = (nr * ar + ni * ai) / den
    z_im = (ni * ar - nr * ai) / den
    br, bi = b_re.astype(F32), b_im.astype(F32)
    bb_re = z_re[..., None] * br - z_im[..., None] * bi
    bb_im = z_re[..., None] * bi + z_im[..., None] * br
    gl = SSM_GROUPS // S5_CB
    eye = jnp.eye(gl, dtype=F32)

    def in_block(m):
        m = m.reshape(S5_CB, gl, SSM_STATE, SSM_GROUP)
        return jnp.einsum('cgpi,gh->cgihp', m, eye).reshape(S5_CB, gl * SSM_GROUP, gl * SSM_STATE)

    def out_block(m):
        m = m.reshape(S5_CB, gl, SSM_GROUP, SSM_STATE)
        return jnp.einsum('cgip,gh->cgphi', m, eye).reshape(S5_CB, gl * SSM_STATE, gl * SSM_GROUP)

    bm = jnp.concatenate([in_block(bb_re), in_block(bb_im)], axis=2).astype(BF16)
    cm = jnp.concatenate([out_block(c_re.astype(F32)), -out_block(c_im.astype(F32))], axis=1).astype(BF16)
    return lam_re.reshape(1, -1), lam_im.reshape(1, -1), bm, cm


def _s5_call(proj, lam_re, lam_im, bm, cm, d_skip, glu_w_bf16, glu_b, norm_w, tt):
    B, S, _ = proj.shape
    R = tt * B
    ucol = (4 * D_HGRN) // D_SSM
    full = lambda shape: pl.BlockSpec(shape, lambda i: (0,) * len(shape))
    return pl.pallas_call(
        functools.partial(_s5_kernel, nb=B, tt=tt),
        grid=(S // tt,),
        in_specs=[pl.BlockSpec((B, tt, D_SSM), lambda i: (0, i, ucol)),
                  full((1, S5_NS)), full((1, S5_NS)),
                  full((S5_CB, LANES, 2 * S5_SB)), full((S5_CB, 2 * S5_SB, LANES)),
                  full((1, D_SSM)), full((D_SSM, D_SSM)), full((1, D_SSM)), full((1, D_SSM))],
        out_specs=pl.BlockSpec((B, tt, D_SSM), lambda i: (0, i, 0)),
        out_shape=jax.ShapeDtypeStruct((B, S, D_SSM), F32),
        scratch_shapes=[pltpu.VMEM((S5_CB, R, LANES), F32), pltpu.VMEM((R, 2 * S5_NS), F32),
                        pltpu.VMEM((B, 2 * S5_NS), F32), pltpu.VMEM((S5_CB, R, LANES), F32)],
        compiler_params=_params(("arbitrary",)),
        name="s5",
    )(proj, lam_re, lam_im, bm, cm, d_skip.reshape(1, -1), glu_w_bf16, glu_b.reshape(1, -1),
      norm_w.reshape(1, -1))


def _layer_norm(y, w, b):
    mu = jnp.mean(y, axis=-1, keepdims=True)
    yc = y - mu
    var = jnp.mean(yc * yc, axis=-1, keepdims=True)
    return yc * lax.rsqrt(var + LN_EPS) * w + b


def _out_kernel(x_ref, hg_ref, ss_ref, g1_ref, sc2_ref, sh2_ref, wo_ref, lw_ref, lb_ref, wq_ref,
                x1_ref, h2_ref, q2_ref):
    mixed = (jnp.dot(hg_ref[0].astype(BF16), wo_ref[:D_HGRN, :], preferred_element_type=F32)
             + jnp.dot(ss_ref[0].astype(BF16), wo_ref[D_HGRN:, :], preferred_element_type=F32))
    x1 = _layer_norm(DEEP_ALPHA * x_ref[0] + (1.0 + g1_ref[0]) * mixed, lw_ref[...], lb_ref[...])
    x1_ref[0] = x1
    h2 = x1 * (1.0 + sc2_ref[0]) + sh2_ref[0]
    h2_ref[0] = h2
    q2_ref[0] = jnp.dot(h2.astype(BF16), wq_ref[...], preferred_element_type=F32)


def _out_call(x, o_hg, o_ssm, g1, sc2, sh2, w_out_bf16, ln_w, ln_b, w_q_bf16, tm):
    B, S, D = x.shape
    NQ = w_q_bf16.shape[1]
    row = lambda n: pl.BlockSpec((1, tm, n), lambda b, i: (b, i, 0))
    mod = pl.BlockSpec((1, 1, D), lambda b, i: (b, 0, 0))
    full = lambda shape: pl.BlockSpec(shape, lambda b, i: (0,) * len(shape))
    return pl.pallas_call(
        _out_kernel,
        grid=(B, S // tm),
        in_specs=[row(D), row(D_HGRN), row(D_SSM), mod, mod, mod,
                  full((D, D)), full((1, D)), full((1, D)), full((D, NQ))],
        out_specs=[row(D), row(D), row(NQ)],
        out_shape=[jax.ShapeDtypeStruct((B, S, D), F32), jax.ShapeDtypeStruct((B, S, D), F32),
                   jax.ShapeDtypeStruct((B, S, NQ), F32)],
        compiler_params=_params(("arbitrary", "arbitrary")),
        name="outproj",
    )(x, o_hg, o_ssm, g1, sc2, sh2, w_out_bf16, ln_w.reshape(1, -1), ln_b.reshape(1, -1), w_q_bf16)


_CAND_PIECES = ((0, 1, 0, 16, 0), (1, 1, 0, 8, 0), (2, 1, 0, 8, 0), (3, 1, 0, 8, 0),
                (8, 8, 0, 1, 0), (0, 8, 0, 1, 4), (0, 8, 1, 1, 4), (0, 8, 2, 1, 4))


def _topk_kernel(q_ref, k_ref, j_ref, p_ref, g_ref, *, tb):
    nk = PEER_NKEYS
    kiota = lax.broadcasted_iota(I32, (nk, tb), 0)
    ninf = jnp.float32(-jnp.inf)

    def top16(sc):
        vals, idxs = [], []
        for _ in range(PEER_TOPK):
            m = jnp.max(sc, axis=0, keepdims=True)
            idx = jnp.min(jnp.where(sc == m, kiota, nk), axis=0, keepdims=True)
            sc = jnp.where(kiota == idx, ninf, sc)
            vals.append(m)
            idxs.append(idx)
        return jnp.concatenate(vals, axis=0), jnp.concatenate(idxs, axis=0)

    flat_rows = []
    for a0, na, b0, nb, _ in _CAND_PIECES:
        for a in range(a0, a0 + na):
            for b in range(b0, b0 + nb):
                flat_rows.append(a * PEER_TOPK + b)
    ncand = len(flat_rows)
    riota = lax.broadcasted_iota(I32, (ncand, 1), 0)
    cflat = jnp.zeros((ncand, 1), I32)
    for rix, fv in enumerate(flat_rows):
        cflat = jnp.where(riota == rix, fv, cflat)

    e_rows, g_rows = [], []
    for h in range(PEER_HEADS):
        tops, topi = [], []
        for p in range(2):
            hp = 2 * h + p
            qhp = q_ref[:, hp * PEER_HALF:(hp + 1) * PEER_HALF].astype(BF16)
            sc = lax.dot_general(k_ref[hp], qhp, (((1,), (1,)), ((), ())),
                                 preferred_element_type=F32)
            v_, i_ = top16(sc)
            tops.append(v_)
            topi.append(i_)
        s1, s2 = tops
        i1, i2 = topi
        cv, ce = [], []
        for a0, na, b0, nb, mask_below in _CAND_PIECES:
            val = s1[a0:a0 + na, :] + s2[b0:b0 + nb, :]
            eid = i1[a0:a0 + na, :] * PEER_NKEYS + i2[b0:b0 + nb, :]
            if mask_below:
                prow = lax.broadcasted_iota(I32, (max(na, nb), 1), 0)
                val = jnp.where(prow < mask_below, ninf, val)
            cv.append(val)
            ce.append(eid)
        cand = jnp.concatenate(cv, axis=0)
        cexp = jnp.concatenate(ce, axis=0)
        bv, be = [], []
        for _ in range(PEER_TOPK):
            m = jnp.max(cand, axis=0, keepdims=True)
            cm = jnp.min(jnp.where(cand == m, cflat, PEER_TOPK * PEER_TOPK), axis=0, keepdims=True)
            sel = cflat == cm
            be.append(jnp.max(jnp.where(sel, cexp, -1), axis=0, keepdims=True))
            cand = jnp.where(sel, ninf, cand)
            bv.append(m)
        best = jnp.concatenate(bv, axis=0)
        ex = jnp.exp(best - best[0:1, :])
        g_rows.append(ex / jnp.sum(ex, axis=0, keepdims=True))
        e_rows.append(jnp.concatenate(be, axis=0))
    expert = jnp.concatenate(e_rows, axis=0)
    gate = jnp.concatenate(g_rows, axis=0)
    j_ref[...] = (expert >> 1) * PEER_ROWS
    p_ref[...] = jnp.transpose((expert & 1).astype(F32))
    g_ref[...] = jnp.transpose(gate)


def _topk_call(q2, keys_bf16, tb):
    T = q2.shape[0]
    out = lambda dt: jax.ShapeDtypeStruct((T, PEER_HK), dt)
    ospec = pl.BlockSpec((tb, PEER_HK), lambda i: (i, 0))
    return pl.pallas_call(
        functools.partial(_topk_kernel, tb=tb),
        grid=(T // tb,),
        in_specs=[pl.BlockSpec((tb, PEER_HEADS * PEER_DKEY), lambda i: (i, 0)),
                  pl.BlockSpec((2 * PEER_HEADS, PEER_NKEYS, PEER_HALF), lambda i: (0, 0, 0))],
        out_specs=[pl.BlockSpec((PEER_HK, tb), lambda i: (0, i)), ospec, ospec],
        out_shape=[jax.ShapeDtypeStruct((PEER_HK, T), I32), out(F32), out(F32)],
        compiler_params=_params(("arbitrary",)),
        name="topk",
    )(q2, keys_bf16)


PEER_ROWS = D_MODEL // LANES
SLAB_ROWS = 2 * PEER_ROWS
SLABS_PER_TILE = 16
N_TILES = PEER_HK // SLABS_PER_TILE
EXP_COLS = PEER_HK * SLAB_ROWS
TOKEN_UNROLL = 8


def _pack_slabs(table):
    t = table.reshape(PEER_EXPERTS // 2, 2, D_MODEL)
    bits = lambda x: lax.bitcast_convert_type(x.astype(BF16).astype(F32), jnp.uint32)
    words = (bits(t[:, 0]) >> 16) | (bits(t[:, 1]) & jnp.uint32(0xFFFF0000))
    return words.reshape(PEER_EXPERTS // 2 * PEER_ROWS, LANES)


def _load_tile(tab_ref, j_sm, t, c, scale=None):
    slabs = []
    for m in range(SLABS_PER_TILE):
        off = pl.multiple_of(j_sm.at[c * SLABS_PER_TILE + m][t], PEER_ROWS)
        slab = pltpu.bitcast(tab_ref[pl.ds(off, PEER_ROWS), :], BF16)
        slabs.append(slab if scale is None else slab * scale)
    return jnp.concatenate(slabs, axis=0)


def _expand_matrix():
    r = lax.broadcasted_iota(I32, (PEER_HK, EXP_COLS), 0)
    c = lax.broadcasted_iota(I32, (PEER_HK, EXP_COLS), 1)
    return jnp.where(c // SLAB_ROWS == r, 1.0, 0.0).astype(BF16)


def _diag_target(par_rep):
    q = lax.broadcasted_iota(I32, par_rep.shape, 1) % SLAB_ROWS
    return jnp.where((q % 2).astype(F32) == par_rep, (q // 2).astype(F32), -1.0)


def _index_prefetch(j_hbm, bufs, sems, tb, compute):
    step = pl.program_id(0)

    def copy(half, at_step):
        cols = pl.ds((2 * at_step + half) * tb, tb)
        return pltpu.make_async_copy(j_hbm.at[:, cols], bufs[half], sems.at[half])

    @pl.when(step == 0)
    def _():
        copy(0, step).start()

    copy(0, step).wait()
    copy(1, step).start()
    compute(bufs[0], 0)
    copy(1, step).wait()

    @pl.when(step + 1 < pl.num_programs(0))
    def _():
        copy(0, step + 1).start()

    compute(bufs[1], 1)


def _peer_u_kernel(j_hbm, h_ref, p_ref, g_ref, u_ref, a_ref, sel_ref, part_ref, zs_ref, ja, jb, sems, *, tb):
    expand = _expand_matrix()
    q = lax.broadcasted_iota(I32, (tb, EXP_COLS), 1) % SLAB_ROWS
    tile_cols = SLABS_PER_TILE * SLAB_ROWS
    slot_of_col = lax.broadcasted_iota(I32, (SLABS_PER_TILE, tile_cols), 1) // SLAB_ROWS
    own_cols = slot_of_col == lax.broadcasted_iota(I32, (SLABS_PER_TILE, tile_cols), 0)

    def finish(row0):
        for k in range(TOKEN_UNROLL):
            part = part_ref[k * PEER_HK:(k + 1) * PEER_HK, :]
            zs_ref[pl.ds(row0 + k, 1), :] = jnp.sum(jnp.transpose(part), axis=0, keepdims=True)

    def half_block(j_sm, half):
        base = half * tb
        par_rep = jnp.dot(p_ref[base:base + tb, :].astype(BF16), expand, preferred_element_type=F32)
        sel_ref[...] = jnp.where((q % 2).astype(F32) == par_rep, 1.0, 0.0)
        part_ref[...] = jnp.zeros_like(part_ref)

        def token(t, k):
            hbits = lax.bitcast_convert_type(h_ref[base + t].astype(BF16).astype(F32), jnp.uint32)
            hh = pltpu.bitcast((hbits >> 16) | hbits, BF16)
            sel = sel_ref[pl.ds(t, 1), :]
            for c in range(N_TILES):
                w = _load_tile(u_ref, j_sm, t, c, scale=hh)
                pick = jnp.where(own_cols, sel[:, c * tile_cols:(c + 1) * tile_cols], 0.0).astype(BF16)
                r0 = k * PEER_HK + c * SLABS_PER_TILE
                part_ref[r0:r0 + SLABS_PER_TILE, :] = jnp.dot(pick, w, preferred_element_type=F32)

        def tokens(i, carry):
            finish(i * TOKEN_UNROLL)
            for k in range(TOKEN_UNROLL):
                token(i * TOKEN_UNROLL + k, k)
            return carry

        lax.fori_loop(0, tb // TOKEN_UNROLL, tokens, 0)
        finish(tb)
        a_ref[base:base + tb, :] = _gelu_exact(zs_ref[TOKEN_UNROLL:, :]) * g_ref[base:base + tb, :]

    _index_prefetch(j_hbm, (ja, jb), sems, tb, half_block)


def _peer_v_kernel(j_hbm, a_in_ref, p_ref, v_ref, o_ref, arep_ref, tgt_ref, ja, jb, sems, *, tb):
    expand = _expand_matrix()
    sub = lax.broadcasted_iota(I32, (PEER_ROWS, EXP_COLS), 0).astype(F32)

    def half_block(j_sm, half):
        base = half * tb
        par_rep = jnp.dot(p_ref[base:base + tb, :].astype(BF16), expand, preferred_element_type=F32)
        tgt_ref[...] = _diag_target(par_rep)
        arep_ref[...] = jnp.dot(a_in_ref[base:base + tb, :].astype(BF16), expand, preferred_element_type=F32)

        def token(t):
            tgt = tgt_ref[pl.ds(t, 1), :]
            am = jnp.where(sub == tgt, arep_ref[pl.ds(t, 1), :], 0.0).astype(BF16)
            acc = jnp.zeros((PEER_ROWS, LANES), F32)
            for c in range(N_TILES):
                w = _load_tile(v_ref, j_sm, t, c)
                acc = acc + jnp.dot(am[:, c * 256:(c + 1) * 256], w, preferred_element_type=F32)
            o_ref[base + t] = acc

        def tokens(i, carry):
            for k in range(TOKEN_UNROLL):
                token(i * TOKEN_UNROLL + k)
            return carry

        lax.fori_loop(0, tb // TOKEN_UNROLL, tokens, 0)

    _index_prefetch(j_hbm, (ja, jb), sems, tb, half_block)


def _table_spec():
    return pl.BlockSpec((PEER_EXPERTS // 2 * PEER_ROWS, LANES), lambda i: (0, 0),
                        pipeline_mode=pl.Buffered(1))


def _index_scratch(tb):
    return [pltpu.SMEM((PEER_HK, tb), I32), pltpu.SMEM((PEER_HK, tb), I32), pltpu.SemaphoreType.DMA((2,))]


def _peer_u_call(jidx, h2, par, gate, u_slabs, tb):
    T = jidx.shape[1]
    row = pl.BlockSpec((2 * tb, PEER_HK), lambda i: (i, 0))
    return pl.pallas_call(
        functools.partial(_peer_u_kernel, tb=tb),
        grid=(T // (2 * tb),),
        in_specs=[pl.BlockSpec(memory_space=pl.ANY),
                  pl.BlockSpec((2 * tb, PEER_ROWS, LANES), lambda i: (i, 0, 0)),
                  row, row, _table_spec()],
        out_specs=row,
        out_shape=jax.ShapeDtypeStruct((T, PEER_HK), F32),
        scratch_shapes=[pltpu.VMEM((tb, EXP_COLS), F32), pltpu.VMEM((TOKEN_UNROLL * PEER_HK, LANES), F32),
                        pltpu.VMEM((tb + TOKEN_UNROLL, PEER_HK), F32)] + _index_scratch(tb),
        compiler_params=_params(("arbitrary",)),
        name="peer_u",
    )(jidx, h2, par, gate, u_slabs)


def _peer_v_call(jidx, a, par, v_slabs, tb):
    T = jidx.shape[1]
    row = pl.BlockSpec((2 * tb, PEER_HK), lambda i: (i, 0))
    return pl.pallas_call(
        functools.partial(_peer_v_kernel, tb=tb),
        grid=(T // (2 * tb),),
        in_specs=[pl.BlockSpec(memory_space=pl.ANY), row, row, _table_spec()],
        out_specs=pl.BlockSpec((2 * tb, PEER_ROWS, LANES), lambda i: (i, 0, 0)),
        out_shape=jax.ShapeDtypeStruct((T, PEER_ROWS, LANES), F32),
        scratch_shapes=[pltpu.VMEM((tb, EXP_COLS), F32), pltpu.VMEM((tb, EXP_COLS), F32)] + _index_scratch(tb),
        compiler_params=_params(("arbitrary",)),
        name="peer_v",
    )(jidx, a, par, v_slabs)


def _ln2_kernel(x_ref, f_ref, g_ref, w_ref, b_ref, o_ref):
    o_ref[0] = _layer_norm(DEEP_ALPHA * x_ref[0] + (1.0 + g_ref[0]) * f_ref[0], w_ref[...], b_ref[...])


def _ln2_call(x1, ffn, g2, ln_w, ln_b, tm):
    B, S, D = x1.shape
    row = pl.BlockSpec((1, tm, D), lambda b, i: (b, i, 0))
    vec = pl.BlockSpec((1, D), lambda b, i: (0, 0))
    return pl.pallas_call(
        _ln2_kernel,
        grid=(B, S // tm),
        in_specs=[row, row, pl.BlockSpec((1, 1, D), lambda b, i: (b, 0, 0)), vec, vec],
        out_specs=row,
        out_shape=jax.ShapeDtypeStruct((B, S, D), F32),
        compiler_params=_params(("arbitrary", "arbitrary")),
        name="ln2",
    )(x1, ffn, g2, ln_w.reshape(1, -1), ln_b.reshape(1, -1))


def _tiles(B, S):
    return dict(inproj=min(S, 512), hgrn=min(S, 256), hgrn_batches=2 if B % 2 == 0 else 1,
                s5=min(S, 32), out=min(S, 256),
                topk=LANES, peer=LANES, ln2=min(S, 512))


def kernel(x, c, ada_w, ada_b, w_in, hg_lower_bound, hg_norm_w, ssm_a_re, ssm_a_im, ssm_log_dt,
           ssm_b_re, ssm_b_im, ssm_c_re, ssm_c_im, ssm_d, ssm_glu_w, ssm_glu_b, ssm_norm_w,
           w_out, ln1_w, ln1_b, peer_w_q, peer_sub_keys, peer_u, peer_v, ln2_w, ln2_b):
    B, S, D = x.shape
    T = B * S
    tl = _tiles(B, S)
    lb_all = jnp.cumsum(jax.nn.softmax(hg_lower_bound.astype(F32), axis=0), axis=0)
    for l in range(DEPTH):
        mod = _mod_call(c, ada_w[l], ada_b[l]).reshape(B, 6, 1, D)
        sh1, sc1, g1, sh2, sc2, g2 = (mod[:, n] for n in range(6))

        proj = _inproj_call(x, sc1, sh1, w_in[l].astype(BF16), tl["inproj"])
        o_hg = _hgrn_call(proj, lb_all[l], hg_norm_w[l], tl["hgrn"], tl["hgrn_batches"])
        lam_re, lam_im, bm, cm = _s5_prepare(ssm_a_re[l], ssm_a_im[l], ssm_log_dt[l], ssm_b_re[l],
                                             ssm_b_im[l], ssm_c_re[l], ssm_c_im[l])
        o_ssm = _s5_call(proj, lam_re, lam_im, bm, cm, ssm_d[l], ssm_glu_w[l].astype(BF16),
                         ssm_glu_b[l], ssm_norm_w[l], tl["s5"])
        x1, h2, q2 = _out_call(x, o_hg, o_ssm, g1, sc2, sh2, w_out[l].astype(BF16), ln1_w[l], ln1_b[l],
                               peer_w_q[l].astype(BF16), tl["out"])

        keys = peer_sub_keys[l].astype(BF16).reshape(2 * PEER_HEADS, PEER_NKEYS, PEER_HALF)
        jidx, par, gate = _topk_call(q2.reshape(T, PEER_HEADS * PEER_DKEY), keys, tl["topk"])
        u_slabs = _pack_slabs(peer_u[l])
        v_slabs = _pack_slabs(peer_v[l])
        a = _peer_u_call(jidx, h2.reshape(T, PEER_ROWS, LANES), par, gate, u_slabs, tl["peer"])
        ffn = _peer_v_call(jidx, a, par, v_slabs, tl["peer"])
        x = _ln2_call(x1, ffn.reshape(B, S, D), g2, ln2_w[l], ln2_b[l], tl["ln2"])
    return x
```

```python
import functools
import math

import jax
import jax.numpy as jnp
from jax import lax
from jax.experimental import pallas as pl
from jax.experimental.pallas import tpu as pltpu

F32 = jnp.float32
BF16 = jnp.bfloat16
I32 = jnp.int32

D_MODEL = 1024
DEPTH = 1
D_HGRN = 512
HG_HEADS = 4
HG_DK = 128
D_SSM = 512
SSM_GROUP = 16
SSM_GROUPS = 32
SSM_STATE = 64
IN_COLS = 4 * D_HGRN + D_SSM
PEER_HEADS = 8
PEER_NKEYS = 128
PEER_EXPERTS = PEER_NKEYS * PEER_NKEYS
PEER_DKEY = 256
PEER_HALF = 128
PEER_TOPK = 16
PEER_HK = PEER_HEADS * PEER_TOPK
DEEP_ALPHA = (2.0 * DEPTH) ** 0.25
LN_EPS = 1e-5
RMS_EPS = 1e-6

LANES = 128
SUBLANES = 8
HG_CHUNK = 16
NEG_BIG = -1e30
VMEM_LIMIT = 48 * 1024 * 1024


def _sigmoid(x):
    return 1.0 / (1.0 + jnp.exp(-x))


def _silu(x):
    return x * _sigmoid(x)


def _gelu_exact(x):
    return 0.5 * x * (1.0 + lax.erf(x * (1.0 / math.sqrt(2.0))))


def _params(sem):
    return pltpu.CompilerParams(dimension_semantics=sem, vmem_limit_bytes=VMEM_LIMIT)


def _mod_kernel(c_ref, w_ref, b_ref, o_ref):
    cond = _silu(c_ref[...])
    o_ref[...] = jnp.dot(cond, w_ref[...], preferred_element_type=F32,
                         precision=lax.Precision.HIGHEST) + b_ref[...]


def _mod_call(c, ada_w, ada_b):
    B, D = c.shape
    n = ada_w.shape[1] // D
    return pl.pallas_call(
        _mod_kernel,
        grid=(n,),
        in_specs=[pl.BlockSpec((B, D), lambda j: (0, 0)),
                  pl.BlockSpec((D, D), lambda j: (0, j)),
                  pl.BlockSpec((1, D), lambda j: (0, j))],
        out_specs=pl.BlockSpec((B, D), lambda j: (0, j)),
        out_shape=jax.ShapeDtypeStruct((B, n * D), F32),
        compiler_params=_params(("arbitrary",)),
        name="mod",
    )(c, ada_w, ada_b.reshape(1, -1))


def _inproj_kernel(x_ref, sc_ref, sh_ref, w_ref, o_ref):
    h = x_ref[0] * (1.0 + sc_ref[0]) + sh_ref[0]
    o_ref[0] = jnp.dot(h.astype(BF16), w_ref[...], preferred_element_type=F32)


def _inproj_call(x, sc1, sh1, w_in_bf16, tm):
    B, S, D = x.shape
    N = w_in_bf16.shape[1]
    return pl.pallas_call(
        _inproj_kernel,
        grid=(B, S // tm),
        in_specs=[pl.BlockSpec((1, tm, D), lambda b, i: (b, i, 0)),
                  pl.BlockSpec((1, 1, D), lambda b, i: (b, 0, 0)),
                  pl.BlockSpec((1, 1, D), lambda b, i: (b, 0, 0)),
                  pl.BlockSpec((D, N), lambda b, i: (0, 0))],
        out_specs=pl.BlockSpec((1, tm, N), lambda b, i: (b, i, 0)),
        out_shape=jax.ShapeDtypeStruct((B, S, N), F32),
        compiler_params=_params(("arbitrary", "arbitrary")),
        name="inproj",
    )(x, sc1, sh1, w_in_bf16)


def _hgrn_kernel(q_ref, f_ref, i_ref, g_ref, lb_ref, nw_ref, o_ref, st_ref, b_ref, *, ts, nbt):
    @pl.when(pl.program_id(1) == 0)
    def _():
        st_ref[...] = jnp.zeros_like(st_ref)

    lb_all = lb_ref[...]
    r = lax.broadcasted_iota(I32, (ts, ts), 0)
    c = lax.broadcasted_iota(I32, (ts, ts), 1)
    tri = jnp.where((r // HG_CHUNK == c // HG_CHUNK) & (c <= r), 1.0, 0.0).astype(BF16)
    for bi in range(nbt):
        lf = jnp.log(lb_all + (1.0 - lb_all) * _sigmoid(f_ref[bi]))
        hi = lf.astype(BF16)
        r1 = lf - hi.astype(F32)
        mid = r1.astype(BF16)
        lo = (r1 - mid.astype(F32)).astype(BF16)
        b_ref[bi] = (jnp.dot(tri, hi, preferred_element_type=F32)
                     + jnp.dot(tri, mid, preferred_element_type=F32)
                     + jnp.dot(tri, lo, preferred_element_type=F32))

    half = HG_CHUNK // 2
    rows = lax.broadcasted_iota(I32, (HG_CHUNK, 1), 0)

    def head_chunk(sl, bi, h):
        hs = slice(h * HG_DK, (h + 1) * HG_DK)
        lb = lb_ref[:, hs]
        q = q_ref[bi, sl, hs]
        v = i_ref[bi, sl, hs]
        g = g_ref[bi, sl, hs]
        qs = _silu(q)
        k = 1.0 - (lb + (1.0 - lb) * _sigmoid(f_ref[bi, sl, hs]))
        b = b_ref[bi, sl, hs]
        bl = b[HG_CHUNK - 1:HG_CHUNK, :]
        qh = qs * jnp.exp(b)
        kh = k * jnp.exp(bl - b)
        st = st_ref[bi, h]
        o = lax.dot_general(qh.astype(BF16), st.astype(BF16), (((1,), (1,)), ((), ())),
                            preferred_element_type=F32)
        o_lo, o_hi = o[:half], o[half:]
        for s in range(HG_CHUNK):
            lo_rows = 0 if s < half else half
            bs, qss, rs = b[lo_rows:], qs[lo_rows:], rows[lo_rows:]
            e = jnp.exp(jnp.where(rs >= s, bs - b[s:s + 1, :], NEG_BIG))
            contrib = jnp.sum(e * qss * k[s:s + 1, :], axis=1, keepdims=True) * v[s:s + 1, :]
            if s < half:
                o_lo = o_lo + contrib[:half]
                o_hi = o_hi + contrib[half:]
            else:
                o_hi = o_hi + contrib
        o = jnp.concatenate([o_lo, o_hi], axis=0)
        ms = jnp.mean(o * o, axis=1, keepdims=True)
        o_ref[bi, sl, hs] = o * lax.rsqrt(ms + RMS_EPS) * nw_ref[:, hs] * _silu(g)
        upd = lax.dot_general(v.astype(BF16), kh.astype(BF16), (((0,), (0,)), ((), ())),
                              preferred_element_type=F32)
        st_ref[bi, h] = st * jnp.exp(bl) + upd

    def chunk(ci, carry):
        sl = pl.ds(pl.multiple_of(ci * HG_CHUNK, HG_CHUNK), HG_CHUNK)
        for bi in range(nbt):
            for h in range(HG_HEADS):
                head_chunk(sl, bi, h)
        return carry

    lax.fori_loop(0, ts // HG_CHUNK, chunk, 0)


def _hgrn_call(proj, lb, norm_w, ts, nbt):
    B, S, _ = proj.shape

    def col(j):
        return pl.BlockSpec((nbt, ts, D_HGRN), lambda b, i, j=j: (b, i, j))

    vec = pl.BlockSpec((1, D_HGRN), lambda b, i: (0, 0))
    return pl.pallas_call(
        functools.partial(_hgrn_kernel, ts=ts, nbt=nbt),
        grid=(B // nbt, S // ts),
        in_specs=[col(0), col(1), col(2), col(3), vec, vec],
        out_specs=pl.BlockSpec((nbt, ts, D_HGRN), lambda b, i: (b, i, 0)),
        out_shape=jax.ShapeDtypeStruct((B, S, D_HGRN), F32),
        scratch_shapes=[pltpu.VMEM((nbt, HG_HEADS, HG_DK, HG_DK), F32), pltpu.VMEM((nbt, ts, D_HGRN), F32)],
        compiler_params=_params(("arbitrary", "arbitrary")),
        name="hgrn",
    )(proj, proj, proj, proj, lb.reshape(1, -1), norm_w.reshape(1, -1))


S5_CB = D_SSM // LANES
S5_NS = SSM_GROUPS * SSM_STATE
S5_SB = S5_NS // S5_CB


def _s5_kernel(u_ref, lre_ref, lim_ref, bm_ref, cm_ref, d_ref, gw_ref, gb_ref, nw_ref, o_ref,
               ut_ref, bu_ref, xs_ref, y_ref, *, nb, tt):
    @pl.when(pl.program_id(0) == 0)
    def _():
        xs_ref[...] = jnp.zeros_like(xs_ref)

    for b in range(nb):
        for cb in range(S5_CB):
            ut_ref[cb, pl.ds(b, tt, stride=nb), :] = u_ref[b, :, cb * LANES:(cb + 1) * LANES]
    for cb in range(S5_CB):
        res = jnp.dot(ut_ref[cb].astype(BF16), bm_ref[cb], preferred_element_type=F32)
        bu_ref[:, cb * S5_SB:(cb + 1) * S5_SB] = res[:, :S5_SB]
        bu_ref[:, S5_NS + cb * S5_SB:S5_NS + (cb + 1) * S5_SB] = res[:, S5_SB:]

    for sb in range(S5_CB):
        re = slice(sb * S5_SB, (sb + 1) * S5_SB)
        im = slice(S5_NS + sb * S5_SB, S5_NS + (sb + 1) * S5_SB)
        lr = lre_ref[:, re]
        li = lim_ref[:, re]

        def step(t, carry, re=re, im=im, lr=lr, li=li):
            xr, xi = carry
            rows = pl.ds(pl.multiple_of(t * nb, nb), nb)
            nxr = lr * xr - li * xi + bu_ref[rows, re]
            nxi = lr * xi + li * xr + bu_ref[rows, im]
            bu_ref[rows, re] = nxr
            bu_ref[rows, im] = nxi
            return nxr, nxi

        xr, xi = lax.fori_loop(0, tt, step, (xs_ref[:, re], xs_ref[:, im]), unroll=4)
        xs_ref[:, re] = xr
        xs_ref[:, im] = xi

    ys = []
    for cb in range(S5_CB):
        xre = bu_ref[:, cb * S5_SB:(cb + 1) * S5_SB].astype(BF16)
        xim = bu_ref[:, S5_NS + cb * S5_SB:S5_NS + (cb + 1) * S5_SB].astype(BF16)
        ys.append(jnp.dot(xre, cm_ref[cb, :S5_SB, :], preferred_element_type=F32)
                  + jnp.dot(xim, cm_ref[cb, S5_SB:, :], preferred_element_type=F32)
                  + d_ref[:, cb * LANES:(cb + 1) * LANES] * ut_ref[cb])
    y = _gelu_exact(jnp.concatenate(ys, axis=1))
    z = jnp.dot(y.astype(BF16), gw_ref[...], preferred_element_type=F32) + gb_ref[...]
    y = y * _sigmoid(z)
    y = y * lax.rsqrt(jnp.mean(y * y, axis=1, keepdims=True) + RMS_EPS) * nw_ref[...]
    for cb in range(S5_CB):
        y_ref[cb] = y[:, cb * LANES:(cb + 1) * LANES]
    for b in range(nb):
        for cb in range(S5_CB):
            o_ref[b, :, cb * LANES:(cb + 1) * LANES] = y_ref[cb, pl.ds(b, tt, stride=nb), :]


def _s5_prepare(a_re, a_im, log_dt, b_re, b_im, c_re, c_im):
    ar, ai = a_re.astype(F32), a_im.astype(F32)
    dt = jnp.exp(log_dt.astype(F32))[:, None]
    mag = jnp.exp(ar * dt)
    lam_re = mag * jnp.cos(ai * dt)
    lam_im = mag * jnp.sin(ai * dt)
    den = ar * ar + ai * ai
    nr, ni = lam_re - 1.0, lam_im
    z_re = (nr * ar + ni * ai) / den
    z_im = (ni * ar - nr * ai) / den
    br, bi = b_re.astype(F32), b_im.astype(F32)
    bb_re = z_re[..., None] * br - z_im[..., None] * bi
    bb_im = z_re[..., None] * bi + z_im[..., None] * br
    gl = SSM_GROUPS // S5_CB
    eye = jnp.eye(gl, dtype=F32)

    def in_block(m):
        m = m.reshape(S5_CB, gl, SSM_STATE, SSM_GROUP)
        return jnp.einsum('cgpi,gh->cgihp', m, eye).reshape(S5_CB, gl * SSM_GROUP, gl * SSM_STATE)

    def out_block(m):
        m = m.reshape(S5_CB, gl, SSM_GROUP, SSM_STATE)
        return jnp.einsum('cgip,gh->cgphi', m, eye).reshape(S5_CB, gl * SSM_STATE, gl * SSM_GROUP)

    bm = jnp.concatenate([in_block(bb_re), in_block(bb_im)], axis=2).astype(BF16)
    cm = jnp.concatenate([out_block(c_re.astype(F32)), -out_block(c_im.astype(F32))], axis=1).astype(BF16)
    return lam_re.reshape(1, -1), lam_im.reshape(1, -1), bm, cm


def _s5_call(proj, lam_re, lam_im, bm, cm, d_skip, glu_w_bf16, glu_b, norm_w, tt):
    B, S, _ = proj.shape
    R = tt * B
    ucol = (4 * D_HGRN) // D_SSM
    full = lambda shape: pl.BlockSpec(shape, lambda i: (0,) * len(shape))
    return pl.pallas_call(
        functools.partial(_s5_kernel, nb=B, tt=tt),
        grid=(S // tt,),
        in_specs=[pl.BlockSpec((B, tt, D_SSM), lambda i: (0, i, ucol)),
                  full((1, S5_NS)), full((1, S5_NS)),
                  full((S5_CB, LANES, 2 * S5_SB)), full((S5_CB, 2 * S5_SB, LANES)),
                  full((1, D_SSM)), full((D_SSM, D_SSM)), full((1, D_SSM)), full((1, D_SSM))],
        out_specs=pl.BlockSpec((B, tt, D_SSM), lambda i: (0, i, 0)),
        out_shape=jax.ShapeDtypeStruct((B, S, D_SSM), F32),
        scratch_shapes=[pltpu.VMEM((S5_CB, R, LANES), F32), pltpu.VMEM((R, 2 * S5_NS), F32),
                        pltpu.VMEM((B, 2 * S5_NS), F32), pltpu.VMEM((S5_CB, R, LANES), F32)],
        compiler_params=_params(("arbitrary",)),
        name="s5",
    )(proj, lam_re, lam_im, bm, cm, d_skip.reshape(1, -1), glu_w_bf16, glu_b.reshape(1, -1),
      norm_w.reshape(1, -1))


def _layer_norm(y, w, b):
    mu = jnp.mean(y, axis=-1, keepdims=True)
    yc = y - mu
    var = jnp.mean(yc * yc, axis=-1, keepdims=True)
    return yc * lax.rsqrt(var + LN_EPS) * w + b


def _out_kernel(x_ref, hg_ref, ss_ref, g1_ref, sc2_ref, sh2_ref, wo_ref, lw_ref, lb_ref, wq_ref,
                x1_ref, h2_ref, q2_ref):
    mixed = (jnp.dot(hg_ref[0].astype(BF16), wo_ref[:D_HGRN, :], preferred_element_type=F32)
             + jnp.dot(ss_ref[0].astype(BF16), wo_ref[D_HGRN:, :], preferred_element_type=F32))
    x1 = _layer_norm(DEEP_ALPHA * x_ref[0] + (1.0 + g1_ref[0]) * mixed, lw_ref[...], lb_ref[...])
    x1_ref[0] = x1
    h2 = x1 * (1.0 + sc2_ref[0]) + sh2_ref[0]
    h2_ref[0] = h2.reshape(h2.shape[0], D_MODEL // LANES, LANES)
    q2_ref[0] = jnp.dot(h2.astype(BF16), wq_ref[...], preferred_element_type=F32)


def _out_call(x, o_hg, o_ssm, g1, sc2, sh2, w_out_bf16, ln_w, ln_b, w_q_bf16, tm):
    B, S, D = x.shape
    NQ = w_q_bf16.shape[1]
    row = lambda n: pl.BlockSpec((1, tm, n), lambda b, i: (b, i, 0))
    mod = pl.BlockSpec((1, 1, D), lambda b, i: (b, 0, 0))
    full = lambda shape: pl.BlockSpec(shape, lambda b, i: (0,) * len(shape))
    return pl.pallas_call(
        _out_kernel,
        grid=(B, S // tm),
        in_specs=[row(D), row(D_HGRN), row(D_SSM), mod, mod, mod,
                  full((D, D)), full((1, D)), full((1, D)), full((D, NQ))],
        out_specs=[row(D), pl.BlockSpec((1, tm, D // LANES, LANES), lambda b, i: (b, i, 0, 0)), row(NQ)],
        out_shape=[jax.ShapeDtypeStruct((B, S, D), F32), jax.ShapeDtypeStruct((B, S, D // LANES, LANES), F32),
                   jax.ShapeDtypeStruct((B, S, NQ), F32)],
        compiler_params=_params(("arbitrary", "arbitrary")),
        name="outproj",
    )(x, o_hg, o_ssm, g1, sc2, sh2, w_out_bf16, ln_w.reshape(1, -1), ln_b.reshape(1, -1), w_q_bf16)


_CAND_PIECES = ((0, 1, 0, 16, 0), (1, 1, 0, 8, 0), (2, 1, 0, 8, 0), (3, 1, 0, 8, 0),
                (8, 8, 0, 1, 0), (0, 8, 0, 1, 4), (0, 8, 1, 1, 4), (0, 8, 2, 1, 4))


def _topk_kernel(q_ref, k_ref, j_ref, p_ref, g_ref, *, tb):
    nk = PEER_NKEYS
    kiota = lax.broadcasted_iota(I32, (nk, tb), 0)
    ninf = jnp.float32(-jnp.inf)

    def top16(sc):
        vals, idxs = [], []
        for _ in range(PEER_TOPK):
            m = jnp.max(sc, axis=0, keepdims=True)
            idx = jnp.min(jnp.where(sc == m, kiota, nk), axis=0, keepdims=True)
            sc = jnp.where(kiota == idx, ninf, sc)
            vals.append(m)
            idxs.append(idx)
        return jnp.concatenate(vals, axis=0), jnp.concatenate(idxs, axis=0)

    flat_rows = []
    for a0, na, b0, nb, _ in _CAND_PIECES:
        for a in range(a0, a0 + na):
            for b in range(b0, b0 + nb):
                flat_rows.append(a * PEER_TOPK + b)
    ncand = len(flat_rows)
    riota = lax.broadcasted_iota(I32, (ncand, 1), 0)
    cflat = jnp.zeros((ncand, 1), I32)
    for rix, fv in enumerate(flat_rows):
        cflat = jnp.where(riota == rix, fv, cflat)

    e_rows, g_rows = [], []
    for h in range(PEER_HEADS):
        tops, topi = [], []
        for p in range(2):
            hp = 2 * h + p
            qhp = q_ref[:, hp * PEER_HALF:(hp + 1) * PEER_HALF].astype(BF16)
            sc = lax.dot_general(k_ref[hp], qhp, (((1,), (1,)), ((), ())),
                                 preferred_element_type=F32)
            v_, i_ = top16(sc)
            tops.append(v_)
            topi.append(i_)
        s1, s2 = tops
        i1, i2 = topi
        cv, ce = [], []
        for a0, na, b0, nb, mask_below in _CAND_PIECES:
            val = s1[a0:a0 + na, :] + s2[b0:b0 + nb, :]
            eid = i1[a0:a0 + na, :] * PEER_NKEYS + i2[b0:b0 + nb, :]
            if mask_below:
                prow = lax.broadcasted_iota(I32, (max(na, nb), 1), 0)
                val = jnp.where(prow < mask_below, ninf, val)
            cv.append(val)
            ce.append(eid)
        cand = jnp.concatenate(cv, axis=0)
        cexp = jnp.concatenate(ce, axis=0)
        bv, be = [], []
        for _ in range(PEER_TOPK):
            m = jnp.max(cand, axis=0, keepdims=True)
            cm = jnp.min(jnp.where(cand == m, cflat, PEER_TOPK * PEER_TOPK), axis=0, keepdims=True)
            sel = cflat == cm
            be.append(jnp.max(jnp.where(sel, cexp, -1), axis=0, keepdims=True))
            cand = jnp.where(sel, ninf, cand)
            bv.append(m)
        best = jnp.concatenate(bv, axis=0)
        ex = jnp.exp(best - best[0:1, :])
        g_rows.append(ex / jnp.sum(ex, axis=0, keepdims=True))
        e_rows.append(jnp.concatenate(be, axis=0))
    expert = jnp.concatenate(e_rows, axis=0)
    gate = jnp.concatenate(g_rows, axis=0)
    j_ref[...] = (expert >> 1) * PEER_ROWS
    p_ref[...] = jnp.transpose((expert & 1).astype(F32))
    g_ref[...] = jnp.transpose(gate)


def _topk_call(q2, keys_bf16, tb):
    T = q2.shape[0]
    out = lambda dt: jax.ShapeDtypeStruct((T, PEER_HK), dt)
    ospec = pl.BlockSpec((tb, PEER_HK), lambda i: (i, 0))
    return pl.pallas_call(
        functools.partial(_topk_kernel, tb=tb),
        grid=(T // tb,),
        in_specs=[pl.BlockSpec((tb, PEER_HEADS * PEER_DKEY), lambda i: (i, 0)),
                  pl.BlockSpec((2 * PEER_HEADS, PEER_NKEYS, PEER_HALF), lambda i: (0, 0, 0))],
        out_specs=[pl.BlockSpec((PEER_HK, tb), lambda i: (0, i)), ospec, ospec],
        out_shape=[jax.ShapeDtypeStruct((PEER_HK, T), I32), out(F32), out(F32)],
        compiler_params=_params(("arbitrary",)),
        name="topk",
    )(q2, keys_bf16)


PEER_ROWS = D_MODEL // LANES
SLAB_ROWS = 2 * PEER_ROWS
SLABS_PER_TILE = 16
N_TILES = PEER_HK // SLABS_PER_TILE
EXP_COLS = PEER_HK * SLAB_ROWS
TOKEN_UNROLL = 16


def _pack_slabs(table):
    t = table.reshape(PEER_EXPERTS // 2, 2, D_MODEL)
    bits = lambda x: lax.bitcast_convert_type(x.astype(BF16).astype(F32), jnp.uint32)
    words = (bits(t[:, 0]) >> 16) | (bits(t[:, 1]) & jnp.uint32(0xFFFF0000))
    return words.reshape(PEER_EXPERTS // 2 * PEER_ROWS, LANES)


def _load_tile(tab_ref, j_sm, t, c, scale=None):
    slabs = []
    for m in range(SLABS_PER_TILE):
        off = pl.multiple_of(j_sm.at[c * SLABS_PER_TILE + m][t], PEER_ROWS)
        slab = pltpu.bitcast(tab_ref[pl.ds(off, PEER_ROWS), :], BF16)
        slabs.append(slab if scale is None else slab * scale)
    return jnp.concatenate(slabs, axis=0)


def _expand_matrix():
    r = lax.broadcasted_iota(I32, (PEER_HK, EXP_COLS), 0)
    c = lax.broadcasted_iota(I32, (PEER_HK, EXP_COLS), 1)
    return jnp.where(c // SLAB_ROWS == r, 1.0, 0.0).astype(BF16)


def _diag_target(par_rep):
    q = lax.broadcasted_iota(I32, par_rep.shape, 1) % SLAB_ROWS
    return jnp.where((q % 2).astype(F32) == par_rep, (q // 2).astype(F32), -1.0)


def _index_prefetch(j_hbm, bufs, sems, tb, compute):
    step = pl.program_id(0)

    def copy(half, at_step):
        cols = pl.ds((2 * at_step + half) * tb, tb)
        return pltpu.make_async_copy(j_hbm.at[:, cols], bufs[half], sems.at[half])

    @pl.when(step == 0)
    def _():
        copy(0, step).start()

    copy(0, step).wait()
    copy(1, step).start()
    compute(bufs[0], 0)
    copy(1, step).wait()

    @pl.when(step + 1 < pl.num_programs(0))
    def _():
        copy(0, step + 1).start()

    compute(bufs[1], 1)


def _peer_u_kernel(j_hbm, h_ref, p_ref, g_ref, u_ref, a_ref, sel_ref, part_ref, zs_ref, ja, jb, sems, *, tb):
    expand = _expand_matrix()
    q = lax.broadcasted_iota(I32, (tb, EXP_COLS), 1) % SLAB_ROWS
    tile_cols = SLABS_PER_TILE * SLAB_ROWS
    slot_of_col = lax.broadcasted_iota(I32, (SLABS_PER_TILE, tile_cols), 1) // SLAB_ROWS
    own_cols = slot_of_col == lax.broadcasted_iota(I32, (SLABS_PER_TILE, tile_cols), 0)

    def finish(row0):
        for k in range(TOKEN_UNROLL):
            part = part_ref[k * PEER_HK:(k + 1) * PEER_HK, :]
            zs_ref[pl.ds(row0 + k, 1), :] = jnp.sum(jnp.transpose(part), axis=0, keepdims=True)

    def half_block(j_sm, half):
        base = half * tb
        par_rep = jnp.dot(p_ref[base:base + tb, :].astype(BF16), expand, preferred_element_type=F32)
        sel_ref[...] = jnp.where((q % 2).astype(F32) == par_rep, 1.0, 0.0)
        part_ref[...] = jnp.zeros_like(part_ref)

        def token(t, k):
            hbits = lax.bitcast_convert_type(h_ref[base + t].astype(BF16).astype(F32), jnp.uint32)
            hh = pltpu.bitcast((hbits >> 16) | hbits, BF16)
            sel = sel_ref[pl.ds(t, 1), :]
            for c in range(N_TILES):
                w = _load_tile(u_ref, j_sm, t, c, scale=hh)
                pick = jnp.where(own_cols, sel[:, c * tile_cols:(c + 1) * tile_cols], 0.0).astype(BF16)
                r0 = k * PEER_HK + c * SLABS_PER_TILE
                part_ref[r0:r0 + SLABS_PER_TILE, :] = jnp.dot(pick, w, preferred_element_type=F32)

        def tokens(i, carry):
            finish(i * TOKEN_UNROLL)
            for k in range(TOKEN_UNROLL):
                token(i * TOKEN_UNROLL + k, k)
            return carry

        lax.fori_loop(0, tb // TOKEN_UNROLL, tokens, 0)
        finish(tb)
        a_ref[base:base + tb, :] = _gelu_exact(zs_ref[TOKEN_UNROLL:, :]) * g_ref[base:base + tb, :]

    _index_prefetch(j_hbm, (ja, jb), sems, tb, half_block)


def _peer_v_kernel(j_hbm, a_in_ref, p_ref, v_ref, x1_ref, g2_ref, lw_ref, lb_ref, o_ref,
                   arep_ref, tgt_ref, ffn_ref, ja, jb, sems, *, tb):
    expand = _expand_matrix()
    sub = lax.broadcasted_iota(I32, (PEER_ROWS, EXP_COLS), 0).astype(F32)

    def half_block(j_sm, half):
        base = half * tb
        par_rep = jnp.dot(p_ref[base:base + tb, :].astype(BF16), expand, preferred_element_type=F32)
        tgt_ref[...] = _diag_target(par_rep)
        arep_ref[...] = jnp.dot(a_in_ref[base:base + tb, :].astype(BF16), expand, preferred_element_type=F32)

        def token(t):
            tgt = tgt_ref[pl.ds(t, 1), :]
            am = jnp.where(sub == tgt, arep_ref[pl.ds(t, 1), :], 0.0).astype(BF16)
            acc = jnp.zeros((PEER_ROWS, LANES), F32)
            for c in range(N_TILES):
                w = _load_tile(v_ref, j_sm, t, c)
                acc = acc + jnp.dot(am[:, c * 256:(c + 1) * 256], w, preferred_element_type=F32)
            ffn_ref[base + t] = acc

        def tokens(i, carry):
            for k in range(TOKEN_UNROLL):
                token(i * TOKEN_UNROLL + k)
            return carry

        lax.fori_loop(0, tb // TOKEN_UNROLL, tokens, 0)

    _index_prefetch(j_hbm, (ja, jb), sems, tb, half_block)
    ffn = ffn_ref[...].reshape(2 * tb, D_MODEL)
    o_ref[...] = _layer_norm(DEEP_ALPHA * x1_ref[...] + (1.0 + g2_ref[0]) * ffn, lw_ref[...], lb_ref[...])


def _table_spec():
    return pl.BlockSpec((PEER_EXPERTS // 2 * PEER_ROWS, LANES), lambda i: (0, 0),
                        pipeline_mode=pl.Buffered(1))


def _index_scratch(tb):
    return [pltpu.SMEM((PEER_HK, tb), I32), pltpu.SMEM((PEER_HK, tb), I32), pltpu.SemaphoreType.DMA((2,))]


def _peer_u_call(jidx, h2, par, gate, u_slabs, tb):
    T = jidx.shape[1]
    row = pl.BlockSpec((2 * tb, PEER_HK), lambda i: (i, 0))
    return pl.pallas_call(
        functools.partial(_peer_u_kernel, tb=tb),
        grid=(T // (2 * tb),),
        in_specs=[pl.BlockSpec(memory_space=pl.ANY),
                  pl.BlockSpec((2 * tb, PEER_ROWS, LANES), lambda i: (i, 0, 0)),
                  row, row, _table_spec()],
        out_specs=row,
        out_shape=jax.ShapeDtypeStruct((T, PEER_HK), F32),
        scratch_shapes=[pltpu.VMEM((tb, EXP_COLS), F32), pltpu.VMEM((TOKEN_UNROLL * PEER_HK, LANES), F32),
                        pltpu.VMEM((tb + TOKEN_UNROLL, PEER_HK), F32)] + _index_scratch(tb),
        compiler_params=_params(("arbitrary",)),
        name="peer_u",
    )(jidx, h2, par, gate, u_slabs)


def _peer_v_call(jidx, a, par, v_slabs, x1, g2, ln_w, ln_b, tb):
    B, S, D = x1.shape
    T = B * S
    rows = 2 * tb
    assert S % rows == 0
    row = pl.BlockSpec((rows, PEER_HK), lambda i: (i, 0))
    wide = pl.BlockSpec((rows, D), lambda i: (i, 0))
    vec = pl.BlockSpec((1, D), lambda i: (0, 0))
    return pl.pallas_call(
        functools.partial(_peer_v_kernel, tb=tb),
        grid=(T // rows,),
        in_specs=[pl.BlockSpec(memory_space=pl.ANY), row, row, _table_spec(), wide,
                  pl.BlockSpec((1, 1, D), lambda i: (i * rows // S, 0, 0)), vec, vec],
        out_specs=wide,
        out_shape=jax.ShapeDtypeStruct((T, D), F32),
        scratch_shapes=[pltpu.VMEM((tb, EXP_COLS), F32), pltpu.VMEM((tb, EXP_COLS), F32),
                        pltpu.VMEM((rows, PEER_ROWS, LANES), F32)] + _index_scratch(tb),
        compiler_params=_params(("arbitrary",)),
        name="peer_v",
    )(jidx, a, par, v_slabs, x1.reshape(T, D), g2, ln_w.reshape(1, -1), ln_b.reshape(1, -1))


def _tiles(B, S):
    return dict(inproj=min(S, 512), hgrn=min(S, 256), hgrn_batches=2 if B % 2 == 0 else 1,
                s5=min(S, 32), out=min(S, 256),
                topk=LANES, peer=LANES, ln2=min(S, 512))


def kernel(x, c, ada_w, ada_b, w_in, hg_lower_bound, hg_norm_w, ssm_a_re, ssm_a_im, ssm_log_dt,
           ssm_b_re, ssm_b_im, ssm_c_re, ssm_c_im, ssm_d, ssm_glu_w, ssm_glu_b, ssm_norm_w,
           w_out, ln1_w, ln1_b, peer_w_q, peer_sub_keys, peer_u, peer_v, ln2_w, ln2_b):
    B, S, D = x.shape
    T = B * S
    tl = _tiles(B, S)
    lb_all = jnp.cumsum(jax.nn.softmax(hg_lower_bound.astype(F32), axis=0), axis=0)
    for l in range(DEPTH):
        mod = _mod_call(c, ada_w[l], ada_b[l]).reshape(B, 6, 1, D)
        sh1, sc1, g1, sh2, sc2, g2 = (mod[:, n] for n in range(6))

        proj = _inproj_call(x, sc1, sh1, w_in[l].astype(BF16), tl["inproj"])
        o_hg = _hgrn_call(proj, lb_all[l], hg_norm_w[l], tl["hgrn"], tl["hgrn_batches"])
        lam_re, lam_im, bm, cm = _s5_prepare(ssm_a_re[l], ssm_a_im[l], ssm_log_dt[l], ssm_b_re[l],
                                             ssm_b_im[l], ssm_c_re[l], ssm_c_im[l])
        o_ssm = _s5_call(proj, lam_re, lam_im, bm, cm, ssm_d[l], ssm_glu_w[l].astype(BF16),
                         ssm_glu_b[l], ssm_norm_w[l], tl["s5"])
        x1, h2, q2 = _out_call(x, o_hg, o_ssm, g1, sc2, sh2, w_out[l].astype(BF16), ln1_w[l], ln1_b[l],
                               peer_w_q[l].astype(BF16), tl["out"])

        keys = peer_sub_keys[l].astype(BF16).reshape(2 * PEER_HEADS, PEER_NKEYS, PEER_HALF)
        jidx, par, gate = _topk_call(q2.reshape(T, PEER_HEADS * PEER_DKEY), keys, tl["topk"])
        u_slabs = _pack_slabs(peer_u[l])
        v_slabs = _pack_slabs(peer_v[l])
        a = _peer_u_call(jidx, h2.reshape(T, PEER_ROWS, LANES), par, gate, u_slabs, tl["peer"])
        x = _peer_v_call(jidx, a, par, v_slabs, x1, g2, ln2_w[l], ln2_b[l], tl["peer"]).reshape(B, S, D)
    return x
```

```python
import functools
import math

import jax
import jax.numpy as jnp
from jax import lax
from jax.experimental import pallas as pl
from jax.experimental.pallas import tpu as pltpu

F32 = jnp.float32
BF16 = jnp.bfloat16
I32 = jnp.int32

D_MODEL = 1024
DEPTH = 1
D_HGRN = 512
HG_HEADS = 4
HG_DK = 128
D_SSM = 512
SSM_GROUP = 16
SSM_GROUPS = 32
SSM_STATE = 64
IN_COLS = 4 * D_HGRN + D_SSM
PEER_HEADS = 8
PEER_NKEYS = 128
PEER_EXPERTS = PEER_NKEYS * PEER_NKEYS
PEER_DKEY = 256
PEER_HALF = 128
PEER_TOPK = 16
PEER_HK = PEER_HEADS * PEER_TOPK
DEEP_ALPHA = (2.0 * DEPTH) ** 0.25
LN_EPS = 1e-5
RMS_EPS = 1e-6

LANES = 128
SUBLANES = 8
HG_CHUNK = 16
NEG_BIG = -1e30
VMEM_LIMIT = 48 * 1024 * 1024


def _sigmoid(x):
    return 1.0 / (1.0 + jnp.exp(-x))


def _silu(x):
    return x * _sigmoid(x)


def _gelu_exact(x):
    return 0.5 * x * (1.0 + lax.erf(x * (1.0 / math.sqrt(2.0))))


def _params(sem):
    return pltpu.CompilerParams(dimension_semantics=sem, vmem_limit_bytes=VMEM_LIMIT)


def _mod_kernel(c_ref, w_ref, b_ref, o_ref):
    cond = _silu(c_ref[...])
    o_ref[...] = jnp.dot(cond, w_ref[...], preferred_element_type=F32,
                         precision=lax.Precision.HIGHEST) + b_ref[...]


def _mod_call(c, ada_w, ada_b):
    B, D = c.shape
    n = ada_w.shape[1] // D
    return pl.pallas_call(
        _mod_kernel,
        grid=(n,),
        in_specs=[pl.BlockSpec((B, D), lambda j: (0, 0)),
                  pl.BlockSpec((D, D), lambda j: (0, j)),
                  pl.BlockSpec((1, D), lambda j: (0, j))],
        out_specs=pl.BlockSpec((B, D), lambda j: (0, j)),
        out_shape=jax.ShapeDtypeStruct((B, n * D), F32),
        compiler_params=_params(("arbitrary",)),
        name="mod",
    )(c, ada_w, ada_b.reshape(1, -1))


def _inproj_kernel(x_ref, sc_ref, sh_ref, w_ref, o_ref):
    h = x_ref[0] * (1.0 + sc_ref[0]) + sh_ref[0]
    o_ref[0] = jnp.dot(h.astype(BF16), w_ref[...], preferred_element_type=F32)


def _inproj_call(x, sc1, sh1, w_in_bf16, tm):
    B, S, D = x.shape
    N = w_in_bf16.shape[1]
    return pl.pallas_call(
        _inproj_kernel,
        grid=(B, S // tm),
        in_specs=[pl.BlockSpec((1, tm, D), lambda b, i: (b, i, 0)),
                  pl.BlockSpec((1, 1, D), lambda b, i: (b, 0, 0)),
                  pl.BlockSpec((1, 1, D), lambda b, i: (b, 0, 0)),
                  pl.BlockSpec((D, N), lambda b, i: (0, 0))],
        out_specs=pl.BlockSpec((1, tm, N), lambda b, i: (b, i, 0)),
        out_shape=jax.ShapeDtypeStruct((B, S, N), F32),
        compiler_params=_params(("arbitrary", "arbitrary")),
        name="inproj",
    )(x, sc1, sh1, w_in_bf16)


def _hgrn_kernel(q_ref, f_ref, i_ref, g_ref, lb_ref, nw_ref, o_ref, st_ref, b_ref, *, ts, nbt):
    @pl.when(pl.program_id(1) == 0)
    def _():
        st_ref[...] = jnp.zeros_like(st_ref)

    lb_all = lb_ref[...]
    r = lax.broadcasted_iota(I32, (ts, ts), 0)
    c = lax.broadcasted_iota(I32, (ts, ts), 1)
    tri = jnp.where((r // HG_CHUNK == c // HG_CHUNK) & (c <= r), 1.0, 0.0).astype(BF16)
    for bi in range(nbt):
        lf = jnp.log(lb_all + (1.0 - lb_all) * _sigmoid(f_ref[bi]))
        hi = lf.astype(BF16)
        r1 = lf - hi.astype(F32)
        mid = r1.astype(BF16)
        lo = (r1 - mid.astype(F32)).astype(BF16)
        b_ref[bi] = (jnp.dot(tri, hi, preferred_element_type=F32)
                     + jnp.dot(tri, mid, preferred_element_type=F32)
                     + jnp.dot(tri, lo, preferred_element_type=F32))

    half = HG_CHUNK // 2
    rows = lax.broadcasted_iota(I32, (HG_CHUNK, 1), 0)

    def head_chunk(sl, bi, h):
        hs = slice(h * HG_DK, (h + 1) * HG_DK)
        lb = lb_ref[:, hs]
        q = q_ref[bi, sl, hs]
        v = i_ref[bi, sl, hs]
        g = g_ref[bi, sl, hs]
        qs = _silu(q)
        k = 1.0 - (lb + (1.0 - lb) * _sigmoid(f_ref[bi, sl, hs]))
        b = b_ref[bi, sl, hs]
        bl = b[HG_CHUNK - 1:HG_CHUNK, :]
        qh = qs * jnp.exp(b)
        kh = k * jnp.exp(bl - b)
        st = st_ref[bi, h]
        o = lax.dot_general(qh.astype(BF16), st.astype(BF16), (((1,), (1,)), ((), ())),
                            preferred_element_type=F32)
        o_lo, o_hi = o[:half], o[half:]
        for s in range(HG_CHUNK):
            lo_rows = 0 if s < half else half
            bs, qss, rs = b[lo_rows:], qs[lo_rows:], rows[lo_rows:]
            e = jnp.exp(jnp.where(rs >= s, bs - b[s:s + 1, :], NEG_BIG))
            contrib = jnp.sum(e * qss * k[s:s + 1, :], axis=1, keepdims=True) * v[s:s + 1, :]
            if s < half:
                o_lo = o_lo + contrib[:half]
                o_hi = o_hi + contrib[half:]
            else:
                o_hi = o_hi + contrib
        o = jnp.concatenate([o_lo, o_hi], axis=0)
        ms = jnp.mean(o * o, axis=1, keepdims=True)
        o_ref[bi, sl, hs] = o * lax.rsqrt(ms + RMS_EPS) * nw_ref[:, hs] * _silu(g)
        upd = lax.dot_general(v.astype(BF16), kh.astype(BF16), (((0,), (0,)), ((), ())),
                              preferred_element_type=F32)
        st_ref[bi, h] = st * jnp.exp(bl) + upd

    def chunk(ci, carry):
        sl = pl.ds(pl.multiple_of(ci * HG_CHUNK, HG_CHUNK), HG_CHUNK)
        for bi in range(nbt):
            for h in range(HG_HEADS):
                head_chunk(sl, bi, h)
        return carry

    lax.fori_loop(0, ts // HG_CHUNK, chunk, 0)


def _hgrn_call(proj, lb, norm_w, ts, nbt):
    B, S, _ = proj.shape

    def col(j):
        return pl.BlockSpec((nbt, ts, D_HGRN), lambda b, i, j=j: (b, i, j))

    vec = pl.BlockSpec((1, D_HGRN), lambda b, i: (0, 0))
    return pl.pallas_call(
        functools.partial(_hgrn_kernel, ts=ts, nbt=nbt),
        grid=(B // nbt, S // ts),
        in_specs=[col(0), col(1), col(2), col(3), vec, vec],
        out_specs=pl.BlockSpec((nbt, ts, D_HGRN), lambda b, i: (b, i, 0)),
        out_shape=jax.ShapeDtypeStruct((B, S, D_HGRN), F32),
        scratch_shapes=[pltpu.VMEM((nbt, HG_HEADS, HG_DK, HG_DK), F32), pltpu.VMEM((nbt, ts, D_HGRN), F32)],
        compiler_params=_params(("arbitrary", "arbitrary")),
        name="hgrn",
    )(proj, proj, proj, proj, lb.reshape(1, -1), norm_w.reshape(1, -1))


S5_CB = D_SSM // LANES
S5_NS = SSM_GROUPS * SSM_STATE
S5_SB = S5_NS // S5_CB


def _s5_kernel(u_ref, lre_ref, lim_ref, bm_ref, cm_ref, d_ref, gw_ref, gb_ref, nw_ref, o_ref,
               ut_ref, bu_ref, xs_ref, y_ref, *, nb, tt):
    @pl.when(pl.program_id(0) == 0)
    def _():
        xs_ref[...] = jnp.zeros_like(xs_ref)

    for b in range(nb):
        for cb in range(S5_CB):
            ut_ref[cb, pl.ds(b, tt, stride=nb), :] = u_ref[b, :, cb * LANES:(cb + 1) * LANES]
    for cb in range(S5_CB):
        res = jnp.dot(ut_ref[cb].astype(BF16), bm_ref[cb], preferred_element_type=F32)
        bu_ref[:, cb * S5_SB:(cb + 1) * S5_SB] = res[:, :S5_SB]
        bu_ref[:, S5_NS + cb * S5_SB:S5_NS + (cb + 1) * S5_SB] = res[:, S5_SB:]

    for sb in range(S5_CB):
        re = slice(sb * S5_SB, (sb + 1) * S5_SB)
        im = slice(S5_NS + sb * S5_SB, S5_NS + (sb + 1) * S5_SB)
        lr = lre_ref[:, re]
        li = lim_ref[:, re]

        def step(t, carry, re=re, im=im, lr=lr, li=li):
            xr, xi = carry
            rows = pl.ds(pl.multiple_of(t * nb, nb), nb)
            nxr = lr * xr - li * xi + bu_ref[rows, re]
            nxi = lr * xi + li * xr + bu_ref[rows, im]
            bu_ref[rows, re] = nxr
            bu_ref[rows, im] = nxi
            return nxr, nxi

        xr, xi = lax.fori_loop(0, tt, step, (xs_ref[:, re], xs_ref[:, im]), unroll=4)
        xs_ref[:, re] = xr
        xs_ref[:, im] = xi

    ys = []
    for cb in range(S5_CB):
        xre = bu_ref[:, cb * S5_SB:(cb + 1) * S5_SB].astype(BF16)
        xim = bu_ref[:, S5_NS + cb * S5_SB:S5_NS + (cb + 1) * S5_SB].astype(BF16)
        ys.append(jnp.dot(xre, cm_ref[cb, :S5_SB, :], preferred_element_type=F32)
                  + jnp.dot(xim, cm_ref[cb, S5_SB:, :], preferred_element_type=F32)
                  + d_ref[:, cb * LANES:(cb + 1) * LANES] * ut_ref[cb])
    y = _gelu_exact(jnp.concatenate(ys, axis=1))
    z = jnp.dot(y.astype(BF16), gw_ref[...], preferred_element_type=F32) + gb_ref[...]
    y = y * _sigmoid(z)
    y = y * lax.rsqrt(jnp.mean(y * y, axis=1, keepdims=True) + RMS_EPS) * nw_ref[...]
    for cb in range(S5_CB):
        y_ref[cb] = y[:, cb * LANES:(cb + 1) * LANES]
    for b in range(nb):
        for cb in range(S5_CB):
            o_ref[b, :, cb * LANES:(cb + 1) * LANES] = y_ref[cb, pl.ds(b, tt, stride=nb), :]


def _s5_prepare(a_re, a_im, log_dt, b_re, b_im, c_re, c_im):
    ar, ai = a_re.astype(F32), a_im.astype(F32)
    dt = jnp.exp(log_dt.astype(F32))[:, None]
    mag = jnp.exp(ar * dt)
    lam_re = mag * jnp.cos(ai * dt)
    lam_im = mag * jnp.sin(ai * dt)
    den = ar * ar + ai * ai
    nr, ni = lam_re - 1.0, lam_im
    z_re = (nr * ar + ni * ai) / den
    z_im = (ni * ar - nr * ai) / den
    br, bi = b_re.astype(F32), b_im.astype(F32)
    bb_re = z_re[..., None] * br - z_im[..., None] * bi
    bb_im = z_re[..., None] * bi + z_im[..., None] * br
    gl = SSM_GROUPS // S5_CB
    eye = jnp.eye(gl, dtype=F32)

    def in_block(m):
        m = m.reshape(S5_CB, gl, SSM_STATE, SSM_GROUP)
        return jnp.einsum('cgpi,gh->cgihp', m, eye).reshape(S5_CB, gl * SSM_GROUP, gl * SSM_STATE)

    def out_block(m):
        m = m.reshape(S5_CB, gl, SSM_GROUP, SSM_STATE)
        return jnp.einsum('cgip,gh->cgphi', m, eye).reshape(S5_CB, gl * SSM_STATE, gl * SSM_GROUP)

    bm = jnp.concatenate([in_block(bb_re), in_block(bb_im)], axis=2).astype(BF16)
    cm = jnp.concatenate([out_block(c_re.astype(F32)), -out_block(c_im.astype(F32))], axis=1).astype(BF16)
    return lam_re.reshape(1, -1), lam_im.reshape(1, -1), bm, cm


def _s5_call(proj, lam_re, lam_im, bm, cm, d_skip, glu_w_bf16, glu_b, norm_w, tt):
    B, S, _ = proj.shape
    R = tt * B
    ucol = (4 * D_HGRN) // D_SSM
    full = lambda shape: pl.BlockSpec(shape, lambda i: (0,) * len(shape))
    return pl.pallas_call(
        functools.partial(_s5_kernel, nb=B, tt=tt),
        grid=(S // tt,),
        in_specs=[pl.BlockSpec((B, tt, D_SSM), lambda i: (0, i, ucol)),
                  full((1, S5_NS)), full((1, S5_NS)),
                  full((S5_CB, LANES, 2 * S5_SB)), full((S5_CB, 2 * S5_SB, LANES)),
                  full((1, D_SSM)), full((D_SSM, D_SSM)), full((1, D_SSM)), full((1, D_SSM))],
        out_specs=pl.BlockSpec((B, tt, D_SSM), lambda i: (0, i, 0)),
        out_shape=jax.ShapeDtypeStruct((B, S, D_SSM), F32),
        scratch_shapes=[pltpu.VMEM((S5_CB, R, LANES), F32), pltpu.VMEM((R, 2 * S5_NS), F32),
                        pltpu.VMEM((B, 2 * S5_NS), F32), pltpu.VMEM((S5_CB, R, LANES), F32)],
        compiler_params=_params(("arbitrary",)),
        name="s5",
    )(proj, lam_re, lam_im, bm, cm, d_skip.reshape(1, -1), glu_w_bf16, glu_b.reshape(1, -1),
      norm_w.reshape(1, -1))


def _layer_norm(y, w, b):
    mu = jnp.mean(y, axis=-1, keepdims=True)
    yc = y - mu
    var = jnp.mean(yc * yc, axis=-1, keepdims=True)
    return yc * lax.rsqrt(var + LN_EPS) * w + b


def _out_kernel(x_ref, hg_ref, ss_ref, g1_ref, sc2_ref, sh2_ref, wo_ref, lw_ref, lb_ref, wq_ref,
                x1_ref, h2_ref, q2_ref):
    mixed = (jnp.dot(hg_ref[0].astype(BF16), wo_ref[:D_HGRN, :], preferred_element_type=F32)
             + jnp.dot(ss_ref[0].astype(BF16), wo_ref[D_HGRN:, :], preferred_element_type=F32))
    x1 = _layer_norm(DEEP_ALPHA * x_ref[0] + (1.0 + g1_ref[0]) * mixed, lw_ref[...], lb_ref[...])
    x1_ref[0] = x1
    h2 = x1 * (1.0 + sc2_ref[0]) + sh2_ref[0]
    h2_ref[0] = h2.reshape(h2.shape[0], D_MODEL // LANES, LANES)
    q2_ref[0] = jnp.dot(h2.astype(BF16), wq_ref[...], preferred_element_type=F32)


def _out_call(x, o_hg, o_ssm, g1, sc2, sh2, w_out_bf16, ln_w, ln_b, w_q_bf16, tm):
    B, S, D = x.shape
    NQ = w_q_bf16.shape[1]
    row = lambda n: pl.BlockSpec((1, tm, n), lambda b, i: (b, i, 0))
    mod = pl.BlockSpec((1, 1, D), lambda b, i: (b, 0, 0))
    full = lambda shape: pl.BlockSpec(shape, lambda b, i: (0,) * len(shape))
    return pl.pallas_call(
        _out_kernel,
        grid=(B, S // tm),
        in_specs=[row(D), row(D_HGRN), row(D_SSM), mod, mod, mod,
                  full((D, D)), full((1, D)), full((1, D)), full((D, NQ))],
        out_specs=[row(D), pl.BlockSpec((1, tm, D // LANES, LANES), lambda b, i: (b, i, 0, 0)), row(NQ)],
        out_shape=[jax.ShapeDtypeStruct((B, S, D), F32), jax.ShapeDtypeStruct((B, S, D // LANES, LANES), F32),
                   jax.ShapeDtypeStruct((B, S, NQ), F32)],
        compiler_params=_params(("arbitrary", "arbitrary")),
        name="outproj",
    )(x, o_hg, o_ssm, g1, sc2, sh2, w_out_bf16, ln_w.reshape(1, -1), ln_b.reshape(1, -1), w_q_bf16)


_CAND_PIECES = ((0, 1, 0, 16, 0), (1, 1, 0, 8, 0), (2, 1, 0, 8, 0), (3, 1, 0, 8, 0),
                (8, 8, 0, 1, 0), (0, 8, 0, 1, 4), (0, 8, 1, 1, 4), (0, 8, 2, 1, 4))


def _topk_kernel(q_ref, k_ref, j_ref, p_ref, g_ref, *, tb):
    nk = PEER_NKEYS
    kiota = lax.broadcasted_iota(I32, (nk, tb), 0)
    ninf = jnp.float32(-jnp.inf)

    def top16(sc):
        vals, idxs = [], []
        for _ in range(PEER_TOPK):
            m = jnp.max(sc, axis=0, keepdims=True)
            idx = jnp.min(jnp.where(sc == m, kiota, nk), axis=0, keepdims=True)
            sc = jnp.where(kiota == idx, ninf, sc)
            vals.append(m)
            idxs.append(idx)
        return jnp.concatenate(vals, axis=0), jnp.concatenate(idxs, axis=0)

    flat_rows = []
    for a0, na, b0, nb, _ in _CAND_PIECES:
        for a in range(a0, a0 + na):
            for b in range(b0, b0 + nb):
                flat_rows.append(a * PEER_TOPK + b)
    ncand = len(flat_rows)
    riota = lax.broadcasted_iota(I32, (ncand, 1), 0)
    cflat = jnp.zeros((ncand, 1), I32)
    for rix, fv in enumerate(flat_rows):
        cflat = jnp.where(riota == rix, fv, cflat)

    e_rows, g_rows = [], []
    for h in range(PEER_HEADS):
        tops, topi = [], []
        for p in range(2):
            hp = 2 * h + p
            qhp = q_ref[:, hp * PEER_HALF:(hp + 1) * PEER_HALF].astype(BF16)
            sc = lax.dot_general(k_ref[hp], qhp, (((1,), (1,)), ((), ())),
                                 preferred_element_type=F32)
            v_, i_ = top16(sc)
            tops.append(v_)
            topi.append(i_)
        s1, s2 = tops
        i1, i2 = topi
        cv, ce = [], []
        for a0, na, b0, nb, mask_below in _CAND_PIECES:
            val = s1[a0:a0 + na, :] + s2[b0:b0 + nb, :]
            eid = i1[a0:a0 + na, :] * PEER_NKEYS + i2[b0:b0 + nb, :]
            if mask_below:
                prow = lax.broadcasted_iota(I32, (max(na, nb), 1), 0)
                val = jnp.where(prow < mask_below, ninf, val)
            cv.append(val)
            ce.append(eid)
        cand = jnp.concatenate(cv, axis=0)
        cexp = jnp.concatenate(ce, axis=0)
        bv, be = [], []
        for _ in range(PEER_TOPK):
            m = jnp.max(cand, axis=0, keepdims=True)
            cm = jnp.min(jnp.where(cand == m, cflat, PEER_TOPK * PEER_TOPK), axis=0, keepdims=True)
            sel = cflat == cm
            be.append(jnp.max(jnp.where(sel, cexp, -1), axis=0, keepdims=True))
            cand = jnp.where(sel, ninf, cand)
            bv.append(m)
        best = jnp.concatenate(bv, axis=0)
        ex = jnp.exp(best - best[0:1, :])
        g_rows.append(ex / jnp.sum(ex, axis=0, keepdims=True))
        e_rows.append(jnp.concatenate(be, axis=0))
    expert = jnp.concatenate(e_rows, axis=0)
    gate = jnp.concatenate(g_rows, axis=0)
    j_ref[...] = (expert >> 1) * PEER_ROWS
    p_ref[...] = jnp.transpose((expert & 1).astype(F32))
    g_ref[...] = jnp.transpose(gate)


def _topk_call(q2, keys_bf16, tb):
    T = q2.shape[0]
    out = lambda dt: jax.ShapeDtypeStruct((T, PEER_HK), dt)
    ospec = pl.BlockSpec((tb, PEER_HK), lambda i: (i, 0))
    return pl.pallas_call(
        functools.partial(_topk_kernel, tb=tb),
        grid=(T // tb,),
        in_specs=[pl.BlockSpec((tb, PEER_HEADS * PEER_DKEY), lambda i: (i, 0)),
                  pl.BlockSpec((2 * PEER_HEADS, PEER_NKEYS, PEER_HALF), lambda i: (0, 0, 0))],
        out_specs=[pl.BlockSpec((PEER_HK, tb), lambda i: (0, i)), ospec, ospec],
        out_shape=[jax.ShapeDtypeStruct((PEER_HK, T), I32), out(F32), out(F32)],
        compiler_params=_params(("arbitrary",)),
        name="topk",
    )(q2, keys_bf16)


PEER_ROWS = D_MODEL // LANES
SLAB_ROWS = 2 * PEER_ROWS
SLABS_PER_TILE = 16
N_TILES = PEER_HK // SLABS_PER_TILE
EXP_COLS = PEER_HK * SLAB_ROWS
TOKEN_UNROLL = 16


PACK_EXPERTS = 512


def _pack_kernel(t_ref, o_ref):
    n = t_ref.shape[0]
    pairs = t_ref[...].reshape(n, PEER_ROWS, LANES).reshape(n // 2, SLAB_ROWS, LANES)
    bits = lambda x: lax.bitcast_convert_type(x.astype(BF16).astype(F32), jnp.uint32)
    words = (bits(pairs[:, :PEER_ROWS]) >> 16) | (bits(pairs[:, PEER_ROWS:]) & jnp.uint32(0xFFFF0000))
    o_ref[...] = words.reshape(n // 2 * PEER_ROWS, LANES)


def _pack_slabs(table):
    E, D = table.shape
    n = min(PACK_EXPERTS, E)
    return pl.pallas_call(
        _pack_kernel,
        grid=(E // n,),
        in_specs=[pl.BlockSpec((n, D), lambda i: (i, 0))],
        out_specs=pl.BlockSpec((n // 2 * PEER_ROWS, LANES), lambda i: (i, 0)),
        out_shape=jax.ShapeDtypeStruct((E // 2 * PEER_ROWS, LANES), jnp.uint32),
        compiler_params=_params(("arbitrary",)),
        name="pack",
    )(table)


def _load_tile(tab_ref, j_sm, t, c, scale=None):
    slabs = []
    for m in range(SLABS_PER_TILE):
        off = pl.multiple_of(j_sm.at[c * SLABS_PER_TILE + m][t], PEER_ROWS)
        slab = pltpu.bitcast(tab_ref[pl.ds(off, PEER_ROWS), :], BF16)
        slabs.append(slab if scale is None else slab * scale)
    return jnp.concatenate(slabs, axis=0)


def _expand_matrix():
    r = lax.broadcasted_iota(I32, (PEER_HK, EXP_COLS), 0)
    c = lax.broadcasted_iota(I32, (PEER_HK, EXP_COLS), 1)
    return jnp.where(c // SLAB_ROWS == r, 1.0, 0.0).astype(BF16)


def _diag_target(par_rep):
    q = lax.broadcasted_iota(I32, par_rep.shape, 1) % SLAB_ROWS
    return jnp.where((q % 2).astype(F32) == par_rep, (q // 2).astype(F32), -1.0)


def _index_prefetch(j_hbm, bufs, sems, tb, compute):
    step = pl.program_id(0)

    def copy(half, at_step):
        cols = pl.ds((2 * at_step + half) * tb, tb)
        return pltpu.make_async_copy(j_hbm.at[:, cols], bufs[half], sems.at[half])

    @pl.when(step == 0)
    def _():
        copy(0, step).start()

    copy(0, step).wait()
    copy(1, step).start()
    compute(bufs[0], 0)
    copy(1, step).wait()

    @pl.when(step + 1 < pl.num_programs(0))
    def _():
        copy(0, step + 1).start()

    compute(bufs[1], 1)


def _peer_u_kernel(j_hbm, h_ref, p_ref, g_ref, u_ref, a_ref, sel_ref, part_ref, zs_ref, ja, jb, sems, *, tb):
    expand = _expand_matrix()
    q = lax.broadcasted_iota(I32, (tb, EXP_COLS), 1) % SLAB_ROWS
    tile_cols = SLABS_PER_TILE * SLAB_ROWS
    slot_of_col = lax.broadcasted_iota(I32, (SLABS_PER_TILE, tile_cols), 1) // SLAB_ROWS
    own_cols = slot_of_col == lax.broadcasted_iota(I32, (SLABS_PER_TILE, tile_cols), 0)

    def finish(row0):
        for k in range(TOKEN_UNROLL):
            part = part_ref[k * PEER_HK:(k + 1) * PEER_HK, :]
            zs_ref[pl.ds(row0 + k, 1), :] = jnp.sum(jnp.transpose(part), axis=0, keepdims=True)

    def half_block(j_sm, half):
        base = half * tb
        par_rep = jnp.dot(p_ref[base:base + tb, :].astype(BF16), expand, preferred_element_type=F32)
        sel_ref[...] = jnp.where((q % 2).astype(F32) == par_rep, 1.0, 0.0)
        part_ref[...] = jnp.zeros_like(part_ref)

        def token(t, k):
            hbits = lax.bitcast_convert_type(h_ref[base + t].astype(BF16).astype(F32), jnp.uint32)
            hh = pltpu.bitcast((hbits >> 16) | hbits, BF16)
            sel = sel_ref[pl.ds(t, 1), :]
            for c in range(N_TILES):
                w = _load_tile(u_ref, j_sm, t, c, scale=hh)
                pick = jnp.where(own_cols, sel[:, c * tile_cols:(c + 1) * tile_cols], 0.0).astype(BF16)
                r0 = k * PEER_HK + c * SLABS_PER_TILE
                part_ref[r0:r0 + SLABS_PER_TILE, :] = jnp.dot(pick, w, preferred_element_type=F32)

        def tokens(i, carry):
            finish(i * TOKEN_UNROLL)
            for k in range(TOKEN_UNROLL):
                token(i * TOKEN_UNROLL + k, k)
            return carry

        lax.fori_loop(0, tb // TOKEN_UNROLL, tokens, 0)
        finish(tb)
        a_ref[base:base + tb, :] = _gelu_exact(zs_ref[TOKEN_UNROLL:, :]) * g_ref[base:base + tb, :]

    _index_prefetch(j_hbm, (ja, jb), sems, tb, half_block)


def _peer_v_kernel(j_hbm, a_in_ref, p_ref, v_ref, x1_ref, g2_ref, lw_ref, lb_ref, o_ref,
                   arep_ref, tgt_ref, ffn_ref, ja, jb, sems, *, tb):
    expand = _expand_matrix()
    sub = lax.broadcasted_iota(I32, (PEER_ROWS, EXP_COLS), 0).astype(F32)

    def half_block(j_sm, half):
        base = half * tb
        par_rep = jnp.dot(p_ref[base:base + tb, :].astype(BF16), expand, preferred_element_type=F32)
        tgt_ref[...] = _diag_target(par_rep)
        arep_ref[...] = jnp.dot(a_in_ref[base:base + tb, :].astype(BF16), expand, preferred_element_type=F32)

        def token(t):
            tgt = tgt_ref[pl.ds(t, 1), :]
            am = jnp.where(sub == tgt, arep_ref[pl.ds(t, 1), :], 0.0).astype(BF16)
            acc = jnp.zeros((PEER_ROWS, LANES), F32)
            for c in range(N_TILES):
                w = _load_tile(v_ref, j_sm, t, c)
                acc = acc + jnp.dot(am[:, c * 256:(c + 1) * 256], w, preferred_element_type=F32)
            ffn_ref[base + t] = acc

        def tokens(i, carry):
            for k in range(TOKEN_UNROLL):
                token(i * TOKEN_UNROLL + k)
            return carry

        lax.fori_loop(0, tb // TOKEN_UNROLL, tokens, 0)

    _index_prefetch(j_hbm, (ja, jb), sems, tb, half_block)
    ffn = ffn_ref[...].reshape(2 * tb, D_MODEL)
    o_ref[...] = _layer_norm(DEEP_ALPHA * x1_ref[...] + (1.0 + g2_ref[0]) * ffn, lw_ref[...], lb_ref[...])


def _table_spec():
    return pl.BlockSpec((PEER_EXPERTS // 2 * PEER_ROWS, LANES), lambda i: (0, 0),
                        pipeline_mode=pl.Buffered(1))


def _index_scratch(tb):
    return [pltpu.SMEM((PEER_HK, tb), I32), pltpu.SMEM((PEER_HK, tb), I32), pltpu.SemaphoreType.DMA((2,))]


def _peer_u_call(jidx, h2, par, gate, u_slabs, tb):
    T = jidx.shape[1]
    row = pl.BlockSpec((2 * tb, PEER_HK), lambda i: (i, 0))
    return pl.pallas_call(
        functools.partial(_peer_u_kernel, tb=tb),
        grid=(T // (2 * tb),),
        in_specs=[pl.BlockSpec(memory_space=pl.ANY),
                  pl.BlockSpec((2 * tb, PEER_ROWS, LANES), lambda i: (i, 0, 0)),
                  row, row, _table_spec()],
        out_specs=row,
        out_shape=jax.ShapeDtypeStruct((T, PEER_HK), F32),
        scratch_shapes=[pltpu.VMEM((tb, EXP_COLS), F32), pltpu.VMEM((TOKEN_UNROLL * PEER_HK, LANES), F32),
                        pltpu.VMEM((tb + TOKEN_UNROLL, PEER_HK), F32)] + _index_scratch(tb),
        compiler_params=_params(("arbitrary",)),
        name="peer_u",
    )(jidx, h2, par, gate, u_slabs)


def _peer_v_call(jidx, a, par, v_slabs, x1, g2, ln_w, ln_b, tb):
    B, S, D = x1.shape
    T = B * S
    rows = 2 * tb
    assert S % rows == 0
    row = pl.BlockSpec((rows, PEER_HK), lambda i: (i, 0))
    wide = pl.BlockSpec((rows, D), lambda i: (i, 0))
    vec = pl.BlockSpec((1, D), lambda i: (0, 0))
    return pl.pallas_call(
        functools.partial(_peer_v_kernel, tb=tb),
        grid=(T // rows,),
        in_specs=[pl.BlockSpec(memory_space=pl.ANY), row, row, _table_spec(), wide,
                  pl.BlockSpec((1, 1, D), lambda i: (i * rows // S, 0, 0)), vec, vec],
        out_specs=wide,
        out_shape=jax.ShapeDtypeStruct((T, D), F32),
        scratch_shapes=[pltpu.VMEM((tb, EXP_COLS), F32), pltpu.VMEM((tb, EXP_COLS), F32),
                        pltpu.VMEM((rows, PEER_ROWS, LANES), F32)] + _index_scratch(tb),
        compiler_params=_params(("arbitrary",)),
        name="peer_v",
    )(jidx, a, par, v_slabs, x1.reshape(T, D), g2, ln_w.reshape(1, -1), ln_b.reshape(1, -1))


def _tiles(B, S):
    return dict(inproj=min(S, 512), hgrn=min(S, 256), hgrn_batches=2 if B % 2 == 0 else 1,
                s5=min(S, 32), out=min(S, 256),
                topk=LANES, peer=LANES, ln2=min(S, 512))


def kernel(x, c, ada_w, ada_b, w_in, hg_lower_bound, hg_norm_w, ssm_a_re, ssm_a_im, ssm_log_dt,
           ssm_b_re, ssm_b_im, ssm_c_re, ssm_c_im, ssm_d, ssm_glu_w, ssm_glu_b, ssm_norm_w,
           w_out, ln1_w, ln1_b, peer_w_q, peer_sub_keys, peer_u, peer_v, ln2_w, ln2_b):
    B, S, D = x.shape
    T = B * S
    tl = _tiles(B, S)
    lb_all = jnp.cumsum(jax.nn.softmax(hg_lower_bound.astype(F32), axis=0), axis=0)
    for l in range(DEPTH):
        mod = _mod_call(c, ada_w[l], ada_b[l]).reshape(B, 6, 1, D)
        sh1, sc1, g1, sh2, sc2, g2 = (mod[:, n] for n in range(6))

        proj = _inproj_call(x, sc1, sh1, w_in[l].astype(BF16), tl["inproj"])
        o_hg = _hgrn_call(proj, lb_all[l], hg_norm_w[l], tl["hgrn"], tl["hgrn_batches"])
        lam_re, lam_im, bm, cm = _s5_prepare(ssm_a_re[l], ssm_a_im[l], ssm_log_dt[l], ssm_b_re[l],
                                             ssm_b_im[l], ssm_c_re[l], ssm_c_im[l])
        o_ssm = _s5_call(proj, lam_re, lam_im, bm, cm, ssm_d[l], ssm_glu_w[l].astype(BF16),
                         ssm_glu_b[l], ssm_norm_w[l], tl["s5"])
        x1, h2, q2 = _out_call(x, o_hg, o_ssm, g1, sc2, sh2, w_out[l].astype(BF16), ln1_w[l], ln1_b[l],
                               peer_w_q[l].astype(BF16), tl["out"])

        keys = peer_sub_keys[l].astype(BF16).reshape(2 * PEER_HEADS, PEER_NKEYS, PEER_HALF)
        jidx, par, gate = _topk_call(q2.reshape(T, PEER_HEADS * PEER_DKEY), keys, tl["topk"])
        u_slabs = _pack_slabs(peer_u[l])
        v_slabs = _pack_slabs(peer_v[l])
        a = _peer_u_call(jidx, h2.reshape(T, PEER_ROWS, LANES), par, gate, u_slabs, tl["peer"])
        x = _peer_v_call(jidx, a, par, v_slabs, x1, g2, ln2_w[l], ln2_b[l], tl["peer"]).reshape(B, S, D)
    return x
```
